```python
import math
import jax, jax.numpy as jnp
from jax import lax
import numpy as np

D_MODEL = 2048
BATCH = 8
SEQ = 4096
DEPTH = 4

DN_HEADS = 16
DN_HEAD_DIM = 128
DN_WIDTH = DN_HEADS * DN_HEAD_DIM
DN_CONV = 4
DN_CHUNK = 64
DA_PATTERNS = ((128, 1), (512, 4), (2048, 16))
DA_GROUPS = len(DA_PATTERNS)
DA_HEADS_PER_GROUP = 8
DA_HEAD_DIM = 128
DA_WIDTH = DA_HEADS_PER_GROUP * DA_HEAD_DIM
ROPE_THETA = 10000.0
N_EXPERTS = 32
TOP_K = 4
D_EXPERT = 512
SWIGLU_LIMIT = 7.0
SWIGLU_ALPHA = 1.702
DEEPNORM_ALPHA = (2 * DEPTH) ** 0.25
DEEPNORM_BETA = (8 * DEPTH) ** -0.25
LN_EPS = 1e-5
NORM_EPS = 1e-6
IN_SIZES = (3 * DN_WIDTH, DN_WIDTH, DN_HEADS, DN_HEADS, 3 * DA_GROUPS * DA_WIDTH, D_MODEL, D_MODEL)
IN_WIDTH = sum(IN_SIZES)

kernel_name = "hybrid_gdn_dilated_moe_deepnorm"


def layer_norm(x, g, b):
    xf = x.astype(jnp.float32)
    mu = jnp.mean(xf, axis=-1, keepdims=True)
    var = jnp.mean(jnp.square(xf - mu), axis=-1, keepdims=True)
    return ((xf - mu) * lax.rsqrt(var + LN_EPS) * g.astype(jnp.float32) + b.astype(jnp.float32)).astype(x.dtype)


def l2_normalize(t):
    return t * lax.rsqrt(jnp.sum(t * t, axis=-1, keepdims=True) + NORM_EPS)


def rotary(t, positions):
    half = t.shape[-1] // 2
    inv_freq = ROPE_THETA ** (-jnp.arange(half, dtype=jnp.float32) / half)
    ang = positions.astype(jnp.float32)[:, None] * inv_freq[None, :]
    cos = jnp.cos(ang)[None, :, None, :]
    sin = jnp.sin(ang)[None, :, None, :]
    tf = t.astype(jnp.float32)
    t1, t2 = tf[..., :half], tf[..., half:]
    return jnp.concatenate([t1 * cos - t2 * sin, t2 * cos + t1 * sin], axis=-1).astype(t.dtype)


def causal_depthwise_conv(x, w):
    K, C = w.shape
    return lax.conv_general_dilated(x, w[:, None, :].astype(x.dtype), window_strides=(1,),
                                    padding=[(K - 1, 0)], dimension_numbers=("NWC", "WIO", "NWC"),
                                    feature_group_count=C)


def gated_delta_rule_chunked(q, k, v, g, beta):
    B, S, H, Dk = q.shape
    Dv = v.shape[-1]
    C = DN_CHUNK
    N = S // C

    def chunks(t):
        return t.reshape(B, N, C, H, t.shape[-1]).transpose(0, 3, 1, 2, 4)

    q, k, v = chunks(q), chunks(k), chunks(v)
    beta = beta.reshape(B, N, C, H).transpose(0, 3, 1, 2)
    g = jnp.cumsum(g.reshape(B, N, C, H).transpose(0, 3, 1, 2), axis=-1)
    causal = jnp.tril(jnp.ones((C, C), dtype=bool))
    strict = jnp.tril(jnp.ones((C, C), dtype=bool), -1)
    decay = jnp.exp(jnp.where(causal, g[..., :, None] - g[..., None, :], -jnp.inf))
    k_beta = k * beta[..., None]
    lower = jnp.where(strict, jnp.einsum("bhncd,bhnmd->bhncm", k_beta, k) * decay, 0.0)
    eye = jnp.eye(C, dtype=jnp.float32)
    unit = eye + lower
    t_inv = lax.linalg.triangular_solve(unit, jnp.broadcast_to(eye, unit.shape), left_side=True,
                                        lower=True, unit_diagonal=True)
    u = jnp.einsum("bhncm,bhnmv->bhncv", t_inv, v * beta[..., None])
    w = jnp.einsum("bhncm,bhnmk->bhnck", t_inv, k_beta * jnp.exp(g)[..., None])
    intra = jnp.einsum("bhncd,bhnmd->bhncm", q, k) * decay

    def step(state, xs):
        q_c, k_c, u_c, w_c, g_c, a_c = xs
        v_new = u_c - jnp.einsum("bhck,bhkv->bhcv", w_c, state)
        o = (jnp.einsum("bhck,bhkv->bhcv", q_c * jnp.exp(g_c)[..., None], state)
             + jnp.einsum("bhcm,bhmv->bhcv", a_c, v_new))
        g_last = g_c[..., -1]
        state = (state * jnp.exp(g_last)[..., None, None]
                 + jnp.einsum("bhck,bhcv->bhkv", k_c * jnp.exp(g_last[..., None] - g_c)[..., None], v_new))
        return state, o

    xs = tuple(jnp.moveaxis(t, 2, 0) for t in (q, k, u, w, g, intra))
    state0 = jnp.zeros((B, H, Dk, Dv), jnp.float32)
    _, o = lax.scan(step, state0, xs)
    return o.transpose(1, 0, 3, 2, 4).reshape(B, S, H, Dv)


def dilated_window_attention(q, k, v, window, dilation):
    B, S, H, Dh = q.shape
    L = window // dilation
    span = dilation * L
    S_pad = -(-S // span) * span
    n = S_pad // dilation
    nb = n // L

    def to_streams(t):
        t = jnp.pad(t, ((0, 0), (0, S_pad - S), (0, 0), (0, 0)))
        return t.reshape(B, n, dilation, H, Dh).transpose(0, 3, 2, 1, 4).reshape(B, H, dilation, nb, L, Dh)

    def with_prev(t):
        prev = jnp.pad(t, ((0, 0), (0, 0), (0, 0), (1, 0), (0, 0), (0, 0)))[:, :, :, :-1]
        return jnp.concatenate([prev, t], axis=4)

    qs = to_streams(q)
    kw = with_prev(to_streams(k))
    vw = with_prev(to_streams(v)).astype(jnp.float32)
    s = jnp.einsum("bhrnqd,bhrnkd->bhrnqk", qs, kw).astype(jnp.float32) * (Dh ** -0.5)
    a = jnp.arange(L)[:, None]
    c = jnp.arange(2 * L)[None, :]
    band = (c >= a) & (c <= a + L)
    blk = jnp.arange(nb)[:, None, None]
    mask = band[None] & ((blk > 0) | (c >= L)[None])
    s = jnp.where(mask, s, -jnp.inf)
    lse = jax.nn.logsumexp(s, axis=-1)
    p = jnp.exp(s - lse[..., None])
    o = jnp.einsum("bhrnqk,bhrnkd->bhrnqd", p, vw)
    o = o.reshape(B, H, dilation, n, Dh).transpose(0, 3, 2, 1, 4).reshape(B, S_pad, H, Dh)[:, :S]
    lse = lse.reshape(B, H, dilation, n).transpose(0, 3, 2, 1).reshape(B, S_pad, H)[:, :S]
    return o, lse


def token_mixer(h, w_in, conv_w, a_log, dt_bias, dn_norm_w, w_branch_a, w_branch_b, w_out):
    B, S, _ = h.shape
    proj = h @ w_in
    offsets = np.cumsum(IN_SIZES)[:-1].tolist()
    qkv_a, z, a_lin, b_lin, qkv_b, gate_a, gate_b = jnp.split(proj, offsets, axis=-1)

    qkv_a = jax.nn.silu(causal_depthwise_conv(qkv_a, conv_w)).astype(jnp.float32)
    qa, ka, va = jnp.split(qkv_a.reshape(B, S, 3, DN_HEADS, DN_HEAD_DIM), 3, axis=2)
    qa = l2_normalize(qa[:, :, 0]) * (DN_HEAD_DIM ** -0.5)
    ka = l2_normalize(ka[:, :, 0])
    va = va[:, :, 0]
    beta = jax.nn.sigmoid(b_lin.astype(jnp.float32))
    g = -jnp.exp(a_log.astype(jnp.float32)) * jax.nn.softplus(a_lin.astype(jnp.float32) + dt_bias.astype(jnp.float32))
    oa = gated_delta_rule_chunked(qa, ka, va, g, beta)
    oa = oa * lax.rsqrt(jnp.mean(oa * oa, axis=-1, keepdims=True) + NORM_EPS) * dn_norm_w.astype(jnp.float32)
    oa = oa * jax.nn.silu(z.astype(jnp.float32).reshape(B, S, DN_HEADS, DN_HEAD_DIM))
    y_a = oa.reshape(B, S, DN_WIDTH).astype(h.dtype)

    qkv_b = qkv_b.reshape(B, S, 3, DA_GROUPS * DA_HEADS_PER_GROUP, DA_HEAD_DIM)
    pos = jnp.arange(S)
    qb = rotary(qkv_b[:, :, 0], pos)
    kb = rotary(qkv_b[:, :, 1], pos)
    vb = qkv_b[:, :, 2]
    outs, lses = [], []
    for gi, (win, dil) in enumerate(DA_PATTERNS):
        hs = slice(gi * DA_HEADS_PER_GROUP, (gi + 1) * DA_HEADS_PER_GROUP)
        o_g, l_g = dilated_window_attention(qb[:, :, hs], kb[:, :, hs], vb[:, :, hs], win, dil)
        outs.append(o_g)
        lses.append(l_g)
    wts = jax.nn.softmax(jnp.stack(lses, axis=0), axis=0)
    y_b = jnp.einsum("gbsh,gbshd->bshd", wts, jnp.stack(outs, axis=0))
    y_b = y_b.reshape(B, S, DA_WIDTH).astype(h.dtype)

    merged = jax.nn.sigmoid(gate_a) * (y_a @ w_branch_a) + jax.nn.sigmoid(gate_b) * (y_b @ w_branch_b)
    return merged @ w_out


def moe_ffn(h, router_w, router_b, w_gate_up, b_gate_up, w_down, b_down):
    B, S, D = h.shape
    t = h.reshape(B * S, D)
    logits = (t @ router_w + router_b).astype(jnp.float32)
    top_val, top_idx = lax.top_k(logits, TOP_K)
    top_w = jax.nn.softmax(top_val, axis=-1)
    combine = jnp.einsum("tk,tke->te", top_w, jax.nn.one_hot(top_idx, N_EXPERTS, dtype=jnp.float32))
    out = jnp.zeros((B * S, D), jnp.float32)
    for e in range(N_EXPERTS):
        gu = t @ w_gate_up[e] + b_gate_up[e]
        gate = jnp.minimum(gu[:, :D_EXPERT], SWIGLU_LIMIT)
        up = jnp.clip(gu[:, D_EXPERT:], -SWIGLU_LIMIT, SWIGLU_LIMIT)
        glu = gate * jax.nn.sigmoid(gate * SWIGLU_ALPHA)
        y = ((up + 1.0) * glu) @ w_down[e] + b_down[e]
        out = out + combine[:, e:e + 1] * y
    return out.astype(h.dtype).reshape(B, S, D)


def setup_inputs(seed: int = 0) -> dict:
    key = jax.random.key(seed)
    ks = jax.random.split(key, 20)
    f32 = jnp.float32
    D = D_MODEL

    def nrm(k, shape, scale):
        return jax.random.normal(k, shape, f32) * scale

    x = nrm(ks[0], (BATCH, SEQ, D), 1.0)
    col_scale = jnp.concatenate([
        jnp.ones((2 * DN_WIDTH,), f32), jnp.full((DN_WIDTH,), DEEPNORM_BETA, f32),
        jnp.ones((DN_WIDTH + 2 * DN_HEADS,), f32),
        jnp.ones((2 * DA_GROUPS * DA_WIDTH,), f32), jnp.full((DA_GROUPS * DA_WIDTH,), DEEPNORM_BETA, f32),
        jnp.ones((2 * D,), f32)])
    w_in = nrm(ks[1], (DEPTH, D, IN_WIDTH), D ** -0.5) * col_scale
    conv_w = nrm(ks[2], (DEPTH, DN_CONV, 3 * DN_WIDTH), DN_CONV ** -0.5)
    a_log = jnp.log(jax.random.uniform(ks[3], (DEPTH, DN_HEADS), f32, 1.0, 16.0))
    dt = jnp.exp(jax.random.uniform(ks[4], (DEPTH, DN_HEADS), f32, math.log(1e-3), math.log(1e-1)))
    dt_bias = dt + jnp.log(-jnp.expm1(-dt))
    dn_norm_w = 1.0 + nrm(ks[5], (DEPTH, DN_HEAD_DIM), 0.02)
    w_branch_a = nrm(ks[6], (DEPTH, DN_WIDTH, D), DN_WIDTH ** -0.5)
    w_branch_b = nrm(ks[7], (DEPTH, DA_WIDTH, D), DA_WIDTH ** -0.5)
    w_out = nrm(ks[8], (DEPTH, D, D), D ** -0.5 * DEEPNORM_BETA)
    ln1_g = 1.0 + nrm(ks[9], (DEPTH, D), 0.02)
    ln1_b = nrm(ks[10], (DEPTH, D), 0.02)
    router_w = nrm(ks[11], (DEPTH, D, N_EXPERTS), D ** -0.5)
    router_b = nrm(ks[12], (DEPTH, N_EXPERTS), 0.01)
    w_gate_up = nrm(ks[13], (DEPTH, N_EXPERTS, D, 2 * D_EXPERT), D ** -0.5 * DEEPNORM_BETA)
    b_gate_up = nrm(ks[14], (DEPTH, N_EXPERTS, 2 * D_EXPERT), 0.01)
    w_down = nrm(ks[15], (DEPTH, N_EXPERTS, D_EXPERT, D), D_EXPERT ** -0.5 * DEEPNORM_BETA)
    b_down = nrm(ks[16], (DEPTH, N_EXPERTS, D), 0.01)
    ln2_g = 1.0 + nrm(ks[17], (DEPTH, D), 0.02)
    ln2_b = nrm(ks[18], (DEPTH, D), 0.02)
    return {"x": x, "w_in": w_in, "conv_w": conv_w, "a_log": a_log, "dt_bias": dt_bias,
            "dn_norm_w": dn_norm_w, "w_branch_a": w_branch_a, "w_branch_b": w_branch_b, "w_out": w_out,
            "ln1_g": ln1_g, "ln1_b": ln1_b, "router_w": router_w, "router_b": router_b,
            "w_gate_up": w_gate_up, "b_gate_up": b_gate_up, "w_down": w_down, "b_down": b_down,
            "ln2_g": ln2_g, "ln2_b": ln2_b}


def reference(x, w_in, conv_w, a_log, dt_bias, dn_norm_w, w_branch_a, w_branch_b, w_out,
              ln1_g, ln1_b, router_w, router_b, w_gate_up, b_gate_up, w_down, b_down, ln2_g, ln2_b):
    for l in range(DEPTH):
        mix = token_mixer(x, w_in[l], conv_w[l], a_log[l], dt_bias[l], dn_norm_w[l],
                          w_branch_a[l], w_branch_b[l], w_out[l])
        x = layer_norm(DEEPNORM_ALPHA * x + mix, ln1_g[l], ln1_b[l])
        ffn = moe_ffn(x, router_w[l], router_b[l], w_gate_up[l], b_gate_up[l], w_down[l], b_down[l])
        x = layer_norm(DEEPNORM_ALPHA * x + ffn, ln2_g[l], ln2_b[l])
    return x
```

```python
import functools

import jax
import jax.numpy as jnp
from jax import lax
from jax.experimental import pallas as pl
from jax.experimental.pallas import tpu as pltpu

F32 = jnp.float32
BF16 = jnp.bfloat16
HIGHEST = lax.Precision.HIGHEST

D_MODEL = 2048
DEPTH = 4
DN_HEADS = 16
DN_HEAD_DIM = 128
DN_WIDTH = DN_HEADS * DN_HEAD_DIM
DN_CONV = 4
DN_CHUNK = 64
DA_PATTERNS = ((128, 1), (512, 4), (2048, 16))
DA_GROUPS = len(DA_PATTERNS)
DA_HEADS_PER_GROUP = 8
DA_HEAD_DIM = 128
DA_WIDTH = DA_HEADS_PER_GROUP * DA_HEAD_DIM
ROPE_THETA = 10000.0
N_EXPERTS = 32
TOP_K = 4
D_EXPERT = 512
SWIGLU_LIMIT = 7.0
SWIGLU_ALPHA = 1.702
DEEPNORM_ALPHA = (2 * DEPTH) ** 0.25
LN_EPS = 1e-5
NORM_EPS = 1e-6

LANES = 128
VMEM_LIMIT = 56 * 1024 * 1024

OFF_QKVZ = 0
OFF_AB = 4 * DN_WIDTH
OFF_QKVB = OFF_AB + 2 * DN_HEADS
OFF_GATES = OFF_QKVB + 3 * DA_GROUPS * DA_WIDTH
IN_WIDTH = OFF_GATES + 2 * D_MODEL


def _params(*sem):
    return pltpu.CompilerParams(dimension_semantics=sem, vmem_limit_bytes=VMEM_LIMIT)


def _sigmoid(v):
    return 1.0 / (1.0 + jnp.exp(-v))


def _dot(a, b):
    return jnp.dot(a, b, preferred_element_type=F32)


def _dot_f32(a, b):
    return jnp.dot(a, b, preferred_element_type=F32, precision=HIGHEST)


def _dot_nt(a, b):
    return lax.dot_general(a, b, (((1,), (1,)), ((), ())), preferred_element_type=F32)


def _dot_tn(a, b):
    return lax.dot_general(a, b, (((0,), (0,)), ((), ())), preferred_element_type=F32)


def _mm_kernel(x_ref, w_ref, o_ref):
    o_ref[...] = _dot(x_ref[...], w_ref[...]).astype(o_ref.dtype)


def _matmul(x, w, out_dtype, tm=1024, tn=1024):
    m, k = x.shape
    n = w.shape[1]
    return pl.pallas_call(
        _mm_kernel,
        grid=(m // tm, n // tn),
        in_specs=[pl.BlockSpec((tm, k), lambda i, j: (i, 0)),
                  pl.BlockSpec((k, tn), lambda i, j: (0, j))],
        out_specs=pl.BlockSpec((tm, tn), lambda i, j: (i, j)),
        out_shape=jax.ShapeDtypeStruct((m, n), out_dtype),
        compiler_params=_params("parallel", "arbitrary"),
        name="matmul",
    )(x, w)


def _mm_rope_kernel(x_ref, w_ref, cos_ref, sin_ref, o_ref, *, n_q_tiles, n_rot_tiles, q_scale):
    j = pl.program_id(1)
    acc = _dot(x_ref[...], w_ref[...])

    @pl.when(j >= n_rot_tiles)
    def _():
        o_ref[...] = acc.astype(o_ref.dtype)

    @pl.when(j < n_rot_tiles)
    def _():
        scale = jnp.where(j < n_q_tiles, q_scale, 1.0).astype(F32)
        cosf = cos_ref[...] * scale
        sinf = sin_ref[...] * scale
        half = DA_HEAD_DIM // 2
        for hh in range(acc.shape[1] // DA_HEAD_DIM):
            cs = slice(hh * DA_HEAD_DIM, (hh + 1) * DA_HEAD_DIM)
            t = acc[:, cs]
            o_ref[:, cs] = (t * cosf + pltpu.roll(t, half, 1) * sinf).astype(o_ref.dtype)


def _matmul_rope(x, w, cosf, sinf, seq, out_dtype, tm=1024, tn=1024):
    m, k = x.shape
    n = w.shape[1]
    tiles_per_part = DA_GROUPS * DA_WIDTH // tn
    nseq = seq // tm
    kern = functools.partial(_mm_rope_kernel, n_q_tiles=tiles_per_part, n_rot_tiles=2 * tiles_per_part,
                             q_scale=DA_HEAD_DIM ** -0.5)
    return pl.pallas_call(
        kern,
        grid=(m // tm, n // tn),
        in_specs=[pl.BlockSpec((tm, k), lambda i, j: (i, 0)),
                  pl.BlockSpec((k, tn), lambda i, j: (0, j)),
                  pl.BlockSpec((tm, DA_HEAD_DIM), lambda i, j: (i % nseq, 0)),
                  pl.BlockSpec((tm, DA_HEAD_DIM), lambda i, j: (i % nseq, 0))],
        out_specs=pl.BlockSpec((tm, tn), lambda i, j: (i, j)),
        out_shape=jax.ShapeDtypeStruct((m, n), out_dtype),
        compiler_params=_params("parallel", "arbitrary"),
        name="matmul_rope",
    )(x, w, cosf, sinf)


def _gates_kernel(x_ref, w_ref, a_ref, dt_ref, o_ref):
    lin = _dot_f32(x_ref[...], w_ref[...])
    xa = lin + dt_ref[...]
    softplus = jnp.maximum(xa, 0.0) + jnp.log(1.0 + jnp.exp(-jnp.abs(xa)))
    g = -jnp.exp(a_ref[...]) * softplus
    beta = _sigmoid(lin)
    lane = lax.broadcasted_iota(jnp.int32, lin.shape, 1)
    o_ref[...] = jnp.where(lane < DN_HEADS, g, beta)


def _dn_gates(x, w_ab, a_log, dt_bias, tm=512):
    m, k = x.shape
    pad = LANES - 2 * DN_HEADS
    w = jnp.pad(w_ab, ((0, 0), (0, pad)))
    a = jnp.pad(a_log.astype(F32), (0, LANES - DN_HEADS)).reshape(1, LANES)
    dt = jnp.pad(dt_bias.astype(F32), (0, LANES - DN_HEADS)).reshape(1, LANES)
    return pl.pallas_call(
        _gates_kernel,
        grid=(m // tm,),
        in_specs=[pl.BlockSpec((tm, k), lambda i: (i, 0)),
                  pl.BlockSpec((k, LANES), lambda i: (0, 0)),
                  pl.BlockSpec((1, LANES), lambda i: (0, 0)),
                  pl.BlockSpec((1, LANES), lambda i: (0, 0))],
        out_specs=pl.BlockSpec((tm, LANES), lambda i: (i, 0)),
        out_shape=jax.ShapeDtypeStruct((m, LANES), F32),
        compiler_params=_params("parallel"),
        name="dn_gates",
    )(x, w, a, dt)


GDN_BLOCK = 512
GDN_HEADS_PER_STEP = 2
CARRY = 8


def _unit_lower_inverse(lmat):
    c = lmat.shape[0]
    ri = lax.broadcasted_iota(jnp.int32, (c, c), 0)
    ci = lax.broadcasted_iota(jnp.int32, (c, c), 1)
    eye = (ri == ci).astype(F32)
    s = 1
    minv = eye
    while s < c:
        shift = s.bit_length()
        off = ((ri >> shift) == (ci >> shift)) & ((ri & s) != 0) & ((ci & s) == 0)
        boff = jnp.where(off, lmat, 0.0)
        if s == 1:
            minv = eye - boff
        else:
            minv = minv - _dot_f32(_dot_f32(minv, boff), minv)
        s *= 2
    return minv


def _gdn_kernel(q_ref, k_ref, v_ref, z_ref, gb_ref, cwq_ref, cwk_ref, cwv_ref, nw_ref, o_ref,
                ext_ref, qkv_ref, gbc_ref, state_ref, *, heads):
    hg = pl.program_id(1)
    t = pl.program_id(2)
    tb = q_ref.shape[0]
    dh = DN_HEAD_DIM
    c = DN_CHUNK

    @pl.when(t == 0)
    def _():
        state_ref[...] = jnp.zeros_like(state_ref)
        ext_ref[:, 0:CARRY, :] = jnp.zeros((3, CARRY, ext_ref.shape[2]), F32)

    for idx, (ref, cw_ref) in enumerate(((q_ref, cwq_ref), (k_ref, cwk_ref), (v_ref, cwv_ref))):
        ext_ref[idx, CARRY:CARRY + tb, :] = ref[...].astype(F32)
        acc = ext_ref[idx, CARRY:CARRY + tb, :] * cw_ref[DN_CONV - 1:DN_CONV, :]
        for back in range(1, DN_CONV):
            acc = acc + (ext_ref[idx, CARRY - back:CARRY - back + tb, :]
                         * cw_ref[DN_CONV - 1 - back:DN_CONV - back, :])
        ext_ref[idx, 0:CARRY, :] = ext_ref[idx, tb:tb + CARRY, :]
        act = acc * _sigmoid(acc)
        for g in range(heads):
            a = act[:, g * dh:(g + 1) * dh]
            if idx < 2:
                a = a * lax.rsqrt(jnp.sum(a * a, axis=-1, keepdims=True) + NORM_EPS)
            if idx == 0:
                a = a * (dh ** -0.5)
            qkv_ref[idx, g] = a

    gb = gb_ref[...]
    lane = lax.broadcasted_iota(jnp.int32, gb.shape, 1)
    for g in range(heads):
        h = hg * heads + g
        gcol = jnp.sum(jnp.where(lane == h, gb, 0.0), axis=-1, keepdims=True)
        bcol = jnp.sum(jnp.where(lane == h + DN_HEADS, gb, 0.0), axis=-1, keepdims=True)
        gbc_ref[0, g] = jnp.broadcast_to(gcol, (tb, dh))
        gbc_ref[1, g] = jnp.broadcast_to(bcol, (tb, dh))

    ri = lax.broadcasted_iota(jnp.int32, (c, c), 0)
    ci = lax.broadcasted_iota(jnp.int32, (c, c), 1)
    tri_incl = (ri >= ci).astype(F32)
    nw = nw_ref[...]

    def chunk(ic, carry):
        r0 = pl.multiple_of(ic * c, c)
        rows = pl.ds(r0, c)
        for g in range(heads):
            q = qkv_ref[0, g, rows, :]
            k = qkv_ref[1, g, rows, :]
            v = qkv_ref[2, g, rows, :]
            gl = gbc_ref[0, g, rows, :]
            beta = gbc_ref[1, g, rows, :]
            gc = _dot_f32(tri_incl, gl)
            gsq = gc[:, :c]
            diff = gsq - gsq.T
            decay = jnp.where(ri >= ci, jnp.exp(jnp.minimum(diff, 0.0)), 0.0)
            kb = k * beta
            kbf = k.astype(BF16)
            both = _dot_nt(jnp.concatenate([q, kb], axis=0).astype(BF16), kbf)
            intra = both[:c] * decay
            lower = jnp.where(ri > ci, both[c:] * decay, 0.0)
            tinv = _unit_lower_inverse(lower)
            eg = jnp.exp(gc)
            rhs = jnp.concatenate([v * beta, kb * eg], axis=1).astype(BF16)
            uw = _dot(tinv.astype(BF16), rhs)
            u = uw[:, :dh]
            w = uw[:, dh:]
            state = state_ref[g]
            sb = state.astype(BF16)
            ws_qs = _dot(jnp.concatenate([w, q * eg], axis=0).astype(BF16), sb)
            v_new = u - ws_qs[:c]
            vnb = v_new.astype(BF16)
            o = ws_qs[c:] + _dot(intra.astype(BF16), vnb)
            g_last = gc[c - 1:c, :]
            kd = (k * jnp.exp(g_last - gc)).astype(BF16)
            state_ref[g] = state * jnp.exp(g_last) + _dot_tn(kd, vnb)
            o = o * lax.rsqrt(jnp.mean(o * o, axis=-1, keepdims=True) + NORM_EPS) * nw
            zz = z_ref[rows, g * dh:(g + 1) * dh].astype(F32)
            o_ref[rows, g * dh:(g + 1) * dh] = (o * (zz * _sigmoid(zz))).astype(o_ref.dtype)
        return carry

    lax.fori_loop(0, tb // c, chunk, 0)


def _gdn(qkvz, gb, conv_w, dn_norm_w, batch, seq):
    tb = GDN_BLOCK
    heads = GDN_HEADS_PER_STEP
    wblk = heads * DN_HEAD_DIM
    nhg = DN_HEADS // heads
    nblk = seq // tb
    tokens = batch * seq

    def col(part):
        return pl.BlockSpec((tb, wblk), lambda b, h, t: (b * nblk + t, part * nhg + h))

    def cw(part):
        return pl.BlockSpec((DN_CONV, wblk), lambda b, h, t: (0, part * nhg + h))

    kern = functools.partial(_gdn_kernel, heads=heads)
    return pl.pallas_call(
        kern,
        grid=(batch, nhg, nblk),
        in_specs=[col(0), col(1), col(2), col(3),
                  pl.BlockSpec((tb, LANES), lambda b, h, t: (b * nblk + t, 0)),
                  cw(0), cw(1), cw(2),
                  pl.BlockSpec((1, DN_HEAD_DIM), lambda b, h, t: (0, 0))],
        out_specs=pl.BlockSpec((tb, wblk), lambda b, h, t: (b * nblk + t, h)),
        out_shape=jax.ShapeDtypeStruct((tokens, DN_WIDTH), BF16),
        scratch_shapes=[pltpu.VMEM((3, tb + CARRY, wblk), F32),
                        pltpu.VMEM((3, heads, tb, DN_HEAD_DIM), F32),
                        pltpu.VMEM((2, heads, tb, DN_HEAD_DIM), F32),
                        pltpu.VMEM((heads, DN_HEAD_DIM, DN_HEAD_DIM), F32)],
        compiler_params=_params("parallel", "parallel", "arbitrary"),
        name="gdn",
    )(qkvz, qkvz, qkvz, qkvz, gb, conv_w, conv_w, conv_w, dn_norm_w.reshape(1, DN_HEAD_DIM).astype(F32))


def _attn_kernel(*refs, first, last, nq, win):
    if first:
        q_ref, k_ref, v_ref = refs[:3]
        rest = refs[3:]
    else:
        q_ref, k_ref, v_ref, acc_in, m_in, l_in = refs[:6]
        rest = refs[6:]
    if last:
        y_ref, kext, vext = rest
    else:
        acc_out, m_out, l_out, kext, vext = rest
    ib = pl.program_id(2)
    rblk = nq * win
    dh = DA_HEAD_DIM

    @pl.when(ib == 0)
    def _():
        kext[0:win, :] = jnp.zeros((win, kext.shape[1]), kext.dtype)
        vext[0:win, :] = jnp.zeros((win, vext.shape[1]), vext.dtype)

    kext[win:win + rblk, :] = k_ref[...]
    vext[win:win + rblk, :] = v_ref[...]

    row = lax.broadcasted_iota(jnp.int32, (win, 2 * win), 0)
    colm = lax.broadcasted_iota(jnp.int32, (win, 2 * win), 1)
    band = (colm >= row) & (colm <= row + win)
    band0 = band & ((ib > 0) | (colm >= win))
    for jq in range(nq):
        mask = band0 if jq == 0 else band
        rows = slice(jq * win, (jq + 1) * win)
        krows = slice(jq * win, (jq + 2) * win)
        for hh in range(DA_HEADS_PER_GROUP):
            cs = slice(hh * dh, (hh + 1) * dh)
            s = _dot_nt(q_ref[rows, cs], kext[krows, cs])
            s = jnp.where(mask, s, -jnp.inf)
            mx = jnp.max(s, axis=-1, keepdims=True)
            if first:
                m_new = mx
                p = jnp.exp(s - m_new)
                l_new = jnp.sum(p, axis=-1, keepdims=True)
                acc = _dot(p.astype(BF16), vext[krows, cs])
            else:
                m_prev = m_in[rows, cs][:, 0:1]
                m_new = jnp.maximum(m_prev, mx)
                alpha = jnp.exp(m_prev - m_new)
                p = jnp.exp(s - m_new)
                l_new = l_in[rows, cs][:, 0:1] * alpha + jnp.sum(p, axis=-1, keepdims=True)
                acc = acc_in[rows, cs] * alpha + _dot(p.astype(BF16), vext[krows, cs])
            if last:
                y_ref[rows, cs] = (acc / l_new).astype(y_ref.dtype)
            else:
                acc_out[rows, cs] = acc
                m_out[rows, cs] = jnp.broadcast_to(m_new, (win, dh))
                l_out[rows, cs] = jnp.broadcast_to(l_new, (win, dh))

    kext[0:win, :] = kext[rblk:rblk + win, :]
    vext[0:win, :] = vext[rblk:rblk + win, :]


def _attn_group(qkvb, state, gi, batch, seq, out_dtype):
    window, dil = DA_PATTERNS[gi]
    win = window // dil
    n = seq // dil
    first = state is None
    last = gi == DA_GROUPS - 1
    nq = min(4, n // win)
    rblk = nq * win
    nb = n // rblk
    wq = 3 * DA_GROUPS * DA_WIDTH
    parts = wq // DA_WIDTH
    tokens = batch * seq
    qv = qkvb.reshape(batch, n, dil * wq)

    def qspec(part):
        return pl.BlockSpec((None, rblk, DA_WIDTH),
                            lambda b, r, i: (b, i, r * parts + part * DA_GROUPS + gi))

    sspec = pl.BlockSpec((None, rblk, DA_WIDTH), lambda b, r, i: (b, i, r))
    in_specs = [qspec(0), qspec(1), qspec(2)]
    args = [qv, qv, qv]
    if not first:
        in_specs += [sspec, sspec, sspec]
        args += [s.reshape(batch, n, dil * DA_WIDTH) for s in state]
    if last:
        out_specs = sspec
        out_shape = jax.ShapeDtypeStruct((batch, n, dil * DA_WIDTH), out_dtype)
    else:
        out_specs = [sspec, sspec, sspec]
        out_shape = [jax.ShapeDtypeStruct((batch, n, dil * DA_WIDTH), F32)] * 3
    kern = functools.partial(_attn_kernel, first=first, last=last, nq=nq, win=win)
    out = pl.pallas_call(
        kern,
        grid=(batch, dil, nb),
        in_specs=in_specs,
        out_specs=out_specs,
        out_shape=out_shape,
        scratch_shapes=[pltpu.VMEM((rblk + win, DA_WIDTH), BF16),
                        pltpu.VMEM((rblk + win, DA_WIDTH), BF16)],
        compiler_params=_params("parallel", "parallel", "arbitrary"),
        name=f"attn_g{gi}",
    )(*args)
    if last:
        return out.reshape(tokens, DA_WIDTH)
    return tuple(o.reshape(tokens, DA_WIDTH) for o in out)


def _merge_kernel(ya_ref, yb_ref, wa_ref, wb_ref, ga_ref, gb_ref, o_ref):
    a = _dot(ya_ref[...], wa_ref[...])
    b = _dot(yb_ref[...], wb_ref[...])
    o_ref[...] = (_sigmoid(ga_ref[...].astype(F32)) * a + _sigmoid(gb_ref[...].astype(F32)) * b).astype(o_ref.dtype)


def _merge(ya, yb, wa, wb, gates, tm=1024, tn=1024):
    m = ya.shape[0]
    n = wa.shape[1]
    nt = n // tn
    return pl.pallas_call(
        _merge_kernel,
        grid=(m // tm, nt),
        in_specs=[pl.BlockSpec((tm, ya.shape[1]), lambda i, j: (i, 0)),
                  pl.BlockSpec((tm, yb.shape[1]), lambda i, j: (i, 0)),
                  pl.BlockSpec((wa.shape[0], tn), lambda i, j: (0, j)),
                  pl.BlockSpec((wb.shape[0], tn), lambda i, j: (0, j)),
                  pl.BlockSpec((tm, tn), lambda i, j: (i, j)),
                  pl.BlockSpec((tm, tn), lambda i, j: (i, nt + j))],
        out_specs=pl.BlockSpec((tm, tn), lambda i, j: (i, j)),
        out_shape=jax.ShapeDtypeStruct((m, n), BF16),
        compiler_params=_params("parallel", "arbitrary"),
        name="merge",
    )(ya, yb, wa, wb, gates, gates)


def _layer_norm(y, g, b):
    mu = jnp.mean(y, axis=-1, keepdims=True)
    yc = y - mu
    var = jnp.mean(yc * yc, axis=-1, keepdims=True)
    return yc * lax.rsqrt(var + LN_EPS) * g + b


def _outproj_kernel(m_ref, w_ref, x_ref, g_ref, b_ref, rw_ref, rb_ref, xo_ref, xb_ref, comb_ref):
    y = DEEPNORM_ALPHA * x_ref[...] + _dot(m_ref[...], w_ref[...])
    x1 = _layer_norm(y, g_ref[...], b_ref[...])
    xo_ref[...] = x1
    xb_ref[...] = x1.astype(xb_ref.dtype)
    logits = _dot_f32(x1, rw_ref[...]) + rb_ref[...]
    lane = lax.broadcasted_iota(jnp.int32, logits.shape, 1)
    work = jnp.where(lane < N_EXPERTS, logits, -jnp.inf)
    top = jnp.max(work, axis=-1, keepdims=True)
    chosen = jnp.zeros(logits.shape, dtype=jnp.bool_)
    for _ in range(TOP_K):
        mx = jnp.max(work, axis=-1, keepdims=True)
        first_idx = jnp.min(jnp.where(work == mx, lane, LANES), axis=-1, keepdims=True)
        sel = lane == first_idx
        chosen = chosen | sel
        work = jnp.where(sel, -jnp.inf, work)
    e = jnp.where(chosen, jnp.exp(logits - top), 0.0)
    comb_ref[...] = e / jnp.sum(e, axis=-1, keepdims=True)


def _outproj_ln_router(merged, w_out, x, ln_g, ln_b, router_w, router_b, tm=512):
    m, d = x.shape
    rw = jnp.pad(router_w.astype(F32), ((0, 0), (0, LANES - N_EXPERTS)))
    rb = jnp.pad(router_b.astype(F32), (0, LANES - N_EXPERTS)).reshape(1, LANES)
    row = lambda i: (i, 0)
    fixed = lambda i: (0, 0)
    return pl.pallas_call(
        _outproj_kernel,
        grid=(m // tm,),
        in_specs=[pl.BlockSpec((tm, d), row),
                  pl.BlockSpec((d, d), fixed),
                  pl.BlockSpec((tm, d), row),
                  pl.BlockSpec((1, d), fixed),
                  pl.BlockSpec((1, d), fixed),
                  pl.BlockSpec((d, LANES), fixed),
                  pl.BlockSpec((1, LANES), fixed)],
        out_specs=[pl.BlockSpec((tm, d), row), pl.BlockSpec((tm, d), row), pl.BlockSpec((tm, LANES), row)],
        out_shape=[jax.ShapeDtypeStruct((m, d), F32), jax.ShapeDtypeStruct((m, d), BF16),
                   jax.ShapeDtypeStruct((m, LANES), F32)],
        compiler_params=_params("parallel"),
        name="outproj_ln_router",
    )(merged, w_out, x, ln_g.reshape(1, d).astype(F32), ln_b.reshape(1, d).astype(F32), rw, rb)


def _moe_kernel(xb_ref, x_ref, comb_ref, wgu_ref, bgu_ref, wd_ref, bd_ref, g_ref, b_ref,
                xo_ref, xbo_ref, acc_ref):
    e = pl.program_id(1)

    @pl.when(e == 0)
    def _():
        acc_ref[...] = jnp.zeros_like(acc_ref)

    gu = _dot(xb_ref[...], wgu_ref[...]) + bgu_ref[...]
    gate = jnp.minimum(gu[:, :D_EXPERT], SWIGLU_LIMIT)
    up = jnp.clip(gu[:, D_EXPERT:], -SWIGLU_LIMIT, SWIGLU_LIMIT)
    glu = gate * _sigmoid(gate * SWIGLU_ALPHA)
    y = _dot(((up + 1.0) * glu).astype(BF16), wd_ref[...]) + bd_ref[...]
    comb = comb_ref[...]
    lane = lax.broadcasted_iota(jnp.int32, comb.shape, 1)
    ce = jnp.sum(jnp.where(lane == e, comb, 0.0), axis=-1, keepdims=True)
    acc_ref[...] += ce * y

    @pl.when(e == pl.num_programs(1) - 1)
    def _():
        x2 = _layer_norm(DEEPNORM_ALPHA * x_ref[...] + acc_ref[...], g_ref[...], b_ref[...])
        xo_ref[...] = x2
        xbo_ref[...] = x2.astype(xbo_ref.dtype)


def _moe_ln(xb, x, comb, wgu, bgu, wd, bd, ln_g, ln_b, tm=512):
    m, d = x.shape
    ne = wgu.shape[0]
    row = lambda i, e: (i, 0)
    fixed = lambda i, e: (0, 0)
    return pl.pallas_call(
        _moe_kernel,
        grid=(m // tm, ne),
        in_specs=[pl.BlockSpec((tm, d), row),
                  pl.BlockSpec((tm, d), row),
                  pl.BlockSpec((tm, LANES), row),
                  pl.BlockSpec((None, d, 2 * D_EXPERT), lambda i, e: (e, 0, 0)),
                  pl.BlockSpec((None, 1, 2 * D_EXPERT), lambda i, e: (e, 0, 0)),
                  pl.BlockSpec((None, D_EXPERT, d), lambda i, e: (e, 0, 0)),
                  pl.BlockSpec((None, 1, d), lambda i, e: (e, 0, 0)),
                  pl.BlockSpec((1, d), fixed),
                  pl.BlockSpec((1, d), fixed)],
        out_specs=[pl.BlockSpec((tm, d), row), pl.BlockSpec((tm, d), row)],
        out_shape=[jax.ShapeDtypeStruct((m, d), F32), jax.ShapeDtypeStruct((m, d), BF16)],
        scratch_shapes=[pltpu.VMEM((tm, d), F32)],
        compiler_params=_params("parallel", "arbitrary"),
        name="moe_ln",
    )(xb, x, comb, wgu, bgu.reshape(ne, 1, -1).astype(F32), wd, bd.reshape(ne, 1, -1).astype(F32),
      ln_g.reshape(1, d).astype(F32), ln_b.reshape(1, d).astype(F32))


def _rope_tables(seq):
    half = DA_HEAD_DIM // 2
    inv_freq = ROPE_THETA ** (-jnp.arange(half, dtype=F32) / half)
    ang = jnp.arange(seq, dtype=F32)[:, None] * inv_freq[None, :]
    cos = jnp.cos(ang)
    sin = jnp.sin(ang)
    return jnp.concatenate([cos, cos], axis=-1), jnp.concatenate([-sin, sin], axis=-1)


def _layer(x, xb, batch, seq, cosf, sinf, w_in, conv_w, a_log, dt_bias, dn_norm_w, w_branch_a, w_branch_b,
           w_out, ln1_g, ln1_b, router_w, router_b, w_gate_up, b_gate_up, w_down, b_down, ln2_g, ln2_b):
    w_qkvz = w_in[:, OFF_QKVZ:OFF_AB].astype(BF16)
    w_ab = w_in[:, OFF_AB:OFF_QKVB]
    w_qkvb = w_in[:, OFF_QKVB:OFF_GATES].astype(BF16)
    w_gates = w_in[:, OFF_GATES:].astype(BF16)

    qkvz = _matmul(xb, w_qkvz, BF16)
    gb = _dn_gates(x, w_ab, a_log, dt_bias)
    qkvb = _matmul_rope(xb, w_qkvb, cosf, sinf, seq, BF16)
    gates = _matmul(xb, w_gates, BF16)

    y_a = _gdn(qkvz, gb, conv_w.astype(F32), dn_norm_w, batch, seq)

    state = None
    for gi in range(DA_GROUPS):
        state = _attn_group(qkvb, state, gi, batch, seq, BF16)
    y_b = state

    merged = _merge(y_a, y_b, w_branch_a.astype(BF16), w_branch_b.astype(BF16), gates)
    x1, x1b, comb = _outproj_ln_router(merged, w_out.astype(BF16), x, ln1_g, ln1_b, router_w, router_b)
    return _moe_ln(x1b, x1, comb, w_gate_up.astype(BF16), b_gate_up, w_down.astype(BF16), b_down, ln2_g, ln2_b)


@jax.jit
def kernel(x, w_in, conv_w, a_log, dt_bias, dn_norm_w, w_branch_a, w_branch_b, w_out, ln1_g, ln1_b,
           router_w, router_b, w_gate_up, b_gate_up, w_down, b_down, ln2_g, ln2_b):
    batch, seq, d = x.shape
    cosf, sinf = _rope_tables(seq)
    xf = x.reshape(batch * seq, d)
    xb = xf.astype(BF16)
    for l in range(w_in.shape[0]):
        xf, xb = _layer(xf, xb, batch, seq, cosf, sinf, w_in[l], conv_w[l], a_log[l], dt_bias[l], dn_norm_w[l],
                        w_branch_a[l], w_branch_b[l], w_out[l], ln1_g[l], ln1_b[l], router_w[l], router_b[l],
                        w_gate_up[l], b_gate_up[l], w_down[l], b_down[l], ln2_g[l], ln2_b[l])
    return xf.reshape(batch, seq, d)
```

```python
import functools

import jax
import jax.numpy as jnp
from jax import lax
from jax.experimental import pallas as pl
from jax.experimental.pallas import tpu as pltpu

F32 = jnp.float32
BF16 = jnp.bfloat16
HIGHEST = lax.Precision.HIGHEST

D_MODEL = 2048
DEPTH = 4
DN_HEADS = 16
DN_HEAD_DIM = 128
DN_WIDTH = DN_HEADS * DN_HEAD_DIM
DN_CONV = 4
DN_CHUNK = 64
DA_PATTERNS = ((128, 1), (512, 4), (2048, 16))
DA_GROUPS = len(DA_PATTERNS)
DA_HEADS_PER_GROUP = 8
DA_HEAD_DIM = 128
DA_WIDTH = DA_HEADS_PER_GROUP * DA_HEAD_DIM
ROPE_THETA = 10000.0
N_EXPERTS = 32
TOP_K = 4
D_EXPERT = 512
SWIGLU_LIMIT = 7.0
SWIGLU_ALPHA = 1.702
DEEPNORM_ALPHA = (2 * DEPTH) ** 0.25
LN_EPS = 1e-5
NORM_EPS = 1e-6

LANES = 128
VMEM_LIMIT = 56 * 1024 * 1024

OFF_QKVZ = 0
OFF_AB = 4 * DN_WIDTH
OFF_QKVB = OFF_AB + 2 * DN_HEADS
OFF_GATES = OFF_QKVB + 3 * DA_GROUPS * DA_WIDTH
IN_WIDTH = OFF_GATES + 2 * D_MODEL


def _params(*sem):
    return pltpu.CompilerParams(dimension_semantics=sem, vmem_limit_bytes=VMEM_LIMIT)


def _sigmoid(v):
    return 1.0 / (1.0 + jnp.exp(-v))


def _dot(a, b):
    return jnp.dot(a, b, preferred_element_type=F32)


def _dot_f32(a, b):
    return jnp.dot(a, b, preferred_element_type=F32, precision=HIGHEST)


def _dot_nt(a, b):
    return lax.dot_general(a, b, (((1,), (1,)), ((), ())), preferred_element_type=F32)


def _dot_tn(a, b):
    return lax.dot_general(a, b, (((0,), (0,)), ((), ())), preferred_element_type=F32)


def _split_bf16(a):
    hi = a.astype(BF16)
    return hi, (a - hi.astype(F32)).astype(BF16)


def _dot_split(a, b):
    ah, al = a
    bh, bl = b
    return _dot(ah, bh) + (_dot(ah, bl) + _dot(al, bh))


def _mm_kernel(x_ref, w_ref, o_ref):
    o_ref[...] = _dot(x_ref[...], w_ref[...]).astype(o_ref.dtype)


def _matmul(x, w, out_dtype, tm=1024, tn=1024):
    m, k = x.shape
    n = w.shape[1]
    return pl.pallas_call(
        _mm_kernel,
        grid=(m // tm, n // tn),
        in_specs=[pl.BlockSpec((tm, k), lambda i, j: (i, 0)),
                  pl.BlockSpec((k, tn), lambda i, j: (0, j))],
        out_specs=pl.BlockSpec((tm, tn), lambda i, j: (i, j)),
        out_shape=jax.ShapeDtypeStruct((m, n), out_dtype),
        compiler_params=_params("parallel", "arbitrary"),
        name="matmul",
    )(x, w)


def _mm_stream_kernel(x_ref, w_ref, cos_ref, sin_ref, o_ref, acc_ref, *, dil, q_scale):
    j = pl.program_id(1)
    tm = x_ref.shape[0]
    rows_per = tm // dil
    half = DA_HEAD_DIM // 2
    acc = _dot(x_ref[...], w_ref[...])
    for hh in range(DA_HEADS_PER_GROUP):
        acc_ref[hh] = acc[:, hh * DA_HEAD_DIM:(hh + 1) * DA_HEAD_DIM]

    def rows(r):
        return pl.ds(r, rows_per, stride=dil) if dil > 1 else slice(None)

    @pl.when(j == 2)
    def _():
        for r in range(dil):
            for hh in range(DA_HEADS_PER_GROUP):
                cs = slice(hh * DA_HEAD_DIM, (hh + 1) * DA_HEAD_DIM)
                o_ref[r, :, cs] = acc_ref[hh, rows(r), :].astype(o_ref.dtype)

    @pl.when(j < 2)
    def _():
        scale = jnp.where(j == 0, q_scale, 1.0).astype(F32)
        for r in range(dil):
            cosf = cos_ref[rows(r), :] * scale
            sinf = sin_ref[rows(r), :] * scale
            for hh in range(DA_HEADS_PER_GROUP):
                cs = slice(hh * DA_HEAD_DIM, (hh + 1) * DA_HEAD_DIM)
                t = acc_ref[hh, rows(r), :]
                o_ref[r, :, cs] = (t * cosf + pltpu.roll(t, half, 1) * sinf).astype(o_ref.dtype)


def _matmul_streams(x, w, cosf, sinf, batch, seq, dil, tm=1024):
    m, k = x.shape
    nseq = seq // tm
    n = seq // dil
    kern = functools.partial(_mm_stream_kernel, dil=dil, q_scale=DA_HEAD_DIM ** -0.5)
    return pl.pallas_call(
        kern,
        grid=(m // tm, 3),
        in_specs=[pl.BlockSpec((tm, k), lambda i, j: (i, 0)),
                  pl.BlockSpec((k, DA_WIDTH), lambda i, j: (0, j)),
                  pl.BlockSpec((tm, DA_HEAD_DIM), lambda i, j: (i % nseq, 0)),
                  pl.BlockSpec((tm, DA_HEAD_DIM), lambda i, j: (i % nseq, 0))],
        out_specs=pl.BlockSpec((None, dil, tm // dil, DA_WIDTH), lambda i, j: (i // nseq, 0, i % nseq, j)),
        out_shape=jax.ShapeDtypeStruct((batch, dil, n, 3 * DA_WIDTH), BF16),
        scratch_shapes=[pltpu.VMEM((DA_HEADS_PER_GROUP, tm, DA_HEAD_DIM), F32)],
        compiler_params=_params("parallel", "arbitrary"),
        name=f"qkv_streams_d{dil}",
    )(x, w, cosf, sinf)


def _gates_kernel(x_ref, w_ref, a_ref, dt_ref, o_ref):
    lin = _dot_f32(x_ref[...], w_ref[...])
    xa = lin + dt_ref[...]
    softplus = jnp.maximum(xa, 0.0) + jnp.log(1.0 + jnp.exp(-jnp.abs(xa)))
    g = -jnp.exp(a_ref[...]) * softplus
    beta = _sigmoid(lin)
    tm = lin.shape[0]
    ri = lax.broadcasted_iota(jnp.int32, (tm, tm), 0)
    ci = lax.broadcasted_iota(jnp.int32, (tm, tm), 1)
    shift = DN_CHUNK.bit_length() - 1
    tri = ((ri >= ci) & ((ri >> shift) == (ci >> shift))).astype(F32)
    gcum = _dot_f32(tri, g)
    lane = lax.broadcasted_iota(jnp.int32, lin.shape, 1)
    o_ref[...] = jnp.where(lane < DN_HEADS, gcum, beta)


def _dn_gates(x, w_ab, a_log, dt_bias, tm=512):
    m, k = x.shape
    pad = LANES - 2 * DN_HEADS
    w = jnp.pad(w_ab, ((0, 0), (0, pad)))
    a = jnp.pad(a_log.astype(F32), (0, LANES - DN_HEADS)).reshape(1, LANES)
    dt = jnp.pad(dt_bias.astype(F32), (0, LANES - DN_HEADS)).reshape(1, LANES)
    return pl.pallas_call(
        _gates_kernel,
        grid=(m // tm,),
        in_specs=[pl.BlockSpec((tm, k), lambda i: (i, 0)),
                  pl.BlockSpec((k, LANES), lambda i: (0, 0)),
                  pl.BlockSpec((1, LANES), lambda i: (0, 0)),
                  pl.BlockSpec((1, LANES), lambda i: (0, 0))],
        out_specs=pl.BlockSpec((tm, LANES), lambda i: (i, 0)),
        out_shape=jax.ShapeDtypeStruct((m, LANES), F32),
        compiler_params=_params("parallel"),
        name="dn_gates",
    )(x, w, a, dt)


GDN_BLOCK = 512
GDN_HEADS_PER_STEP = 4
CARRY = 8


def _bdot(a, b):
    return lax.dot_general(a, b, (((2,), (1,)), ((0,), (0,))), preferred_element_type=F32)


def _bdot_nt(a, b):
    return lax.dot_general(a, b, (((2,), (2,)), ((0,), (0,))), preferred_element_type=F32)


def _bdot_split(a, b):
    ah, al = a
    bh, bl = b
    return _bdot(ah, bh) + (_bdot(ah, bl) + _bdot(al, bh))


def _unit_lower_inverse(lmat):
    c = lmat.shape[-1]
    ri = lax.broadcasted_iota(jnp.int32, lmat.shape, 1)
    ci = lax.broadcasted_iota(jnp.int32, lmat.shape, 2)
    eye = (ri == ci).astype(F32)
    s = 1
    minv = eye
    while s < c:
        shift = s.bit_length()
        off = ((ri >> shift) == (ci >> shift)) & ((ri & s) != 0) & ((ci & s) == 0)
        boff = jnp.where(off, lmat, 0.0)
        if s == 1:
            minv = eye - boff
        else:
            ms = _split_bf16(minv)
            minv = minv - _bdot_split(_split_bf16(_bdot_split(ms, _split_bf16(boff))), ms)
        s *= 2
    return minv


def _gdn_kernel(q_ref, k_ref, v_ref, z_ref, gb_ref, cwq_ref, cwk_ref, cwv_ref, nw_ref, o_ref,
                ext_ref, qkv_ref, gbc_ref, state_ref, u_ref, wq_ref, intra_ref, kd_ref, egl_ref, *, heads):
    hg = pl.program_id(1)
    t = pl.program_id(2)
    tb = q_ref.shape[0]
    dh = DN_HEAD_DIM
    c = DN_CHUNK
    nchunks = tb // c

    @pl.when(t == 0)
    def _():
        state_ref[...] = jnp.zeros_like(state_ref)
        ext_ref[:, 0:CARRY, :] = jnp.zeros((3, CARRY, ext_ref.shape[2]), F32)

    for idx, (ref, cw_ref) in enumerate(((q_ref, cwq_ref), (k_ref, cwk_ref), (v_ref, cwv_ref))):
        ext_ref[idx, CARRY:CARRY + tb, :] = ref[...].astype(F32)
        acc = ext_ref[idx, CARRY:CARRY + tb, :] * cw_ref[DN_CONV - 1:DN_CONV, :]
        for back in range(1, DN_CONV):
            acc = acc + (ext_ref[idx, CARRY - back:CARRY - back + tb, :]
                         * cw_ref[DN_CONV - 1 - back:DN_CONV - back, :])
        ext_ref[idx, 0:CARRY, :] = ext_ref[idx, tb:tb + CARRY, :]
        act = acc * _sigmoid(acc)
        for g in range(heads):
            a = act[:, g * dh:(g + 1) * dh]
            if idx < 2:
                a = a * lax.rsqrt(jnp.sum(a * a, axis=-1, keepdims=True) + NORM_EPS)
            if idx == 0:
                a = a * (dh ** -0.5)
            qkv_ref[idx, g] = a.reshape(nchunks, c, dh)

    gb = gb_ref[...]
    lane = lax.broadcasted_iota(jnp.int32, gb.shape, 1)
    for g in range(heads):
        h = hg * heads + g
        gcol = jnp.sum(jnp.where(lane == h, gb, 0.0), axis=-1, keepdims=True)
        bcol = jnp.sum(jnp.where(lane == h + DN_HEADS, gb, 0.0), axis=-1, keepdims=True)
        gbc_ref[0, g] = jnp.broadcast_to(gcol, (tb, dh)).reshape(nchunks, c, dh)
        gbc_ref[1, g] = jnp.broadcast_to(bcol, (tb, dh)).reshape(nchunks, c, dh)

    ri = lax.broadcasted_iota(jnp.int32, (nchunks, c, c), 1)
    ci = lax.broadcasted_iota(jnp.int32, (nchunks, c, c), 2)

    def head_body(g, carry):
        q = qkv_ref[0, g]
        k = qkv_ref[1, g]
        v = qkv_ref[2, g]
        gc = gbc_ref[0, g]
        beta = gbc_ref[1, g]
        gsq = gc[:, :, :c]
        diff = gsq - jnp.swapaxes(gsq, 1, 2)
        decay = jnp.where(ri >= ci, jnp.exp(jnp.minimum(diff, 0.0)), 0.0)
        kb = k * beta
        both = _bdot_nt(jnp.concatenate([q, kb], axis=1).astype(BF16), k.astype(BF16))
        lower = jnp.where(ri > ci, both[:, c:] * decay, 0.0)
        tinv = _unit_lower_inverse(lower)
        eg = jnp.exp(gc)
        rhs = jnp.concatenate([v * beta, kb * eg], axis=2).astype(BF16)
        uw = _bdot(tinv.astype(BF16), rhs)
        u_ref[g] = uw[:, :, :dh]
        wq_ref[g] = jnp.concatenate([uw[:, :, dh:], q * eg], axis=1).astype(BF16)
        intra_ref[g] = (both[:, :c] * decay).astype(BF16)
        g_last = gc[:, c - 1:c, :]
        kd_ref[g] = (k * jnp.exp(g_last - gc)).astype(BF16)
        egl_ref[g] = jnp.broadcast_to(jnp.exp(g_last), (nchunks, CARRY, dh))
        return carry

    lax.fori_loop(0, heads, head_body, 0)

    nw = nw_ref[...]

    def chunk_body(ic, carry):
        rows = pl.ds(pl.multiple_of(ic * c, c), c)
        states = [state_ref[g] for g in range(heads)]
        outs = []
        for g in range(heads):
            ws_qs = _dot(wq_ref[g, ic], states[g].astype(BF16))
            vnb = (u_ref[g, ic] - ws_qs[:c]).astype(BF16)
            o = ws_qs[c:] + _dot(intra_ref[g, ic], vnb)
            states[g] = states[g] * egl_ref[g, ic, 0:1, :] + _dot_tn(kd_ref[g, ic], vnb)
            o = o * lax.rsqrt(jnp.mean(o * o, axis=-1, keepdims=True) + NORM_EPS) * nw
            zz = z_ref[rows, g * dh:(g + 1) * dh].astype(F32)
            outs.append((o * (zz * _sigmoid(zz))).astype(o_ref.dtype))
        for g in range(heads):
            state_ref[g] = states[g]
            o_ref[rows, g * dh:(g + 1) * dh] = outs[g]
        return carry

    lax.fori_loop(0, nchunks, chunk_body, 0)


def _gdn(qkvz, gb, conv_w, dn_norm_w, batch, seq):
    tb = GDN_BLOCK
    heads = GDN_HEADS_PER_STEP
    wblk = heads * DN_HEAD_DIM
    nhg = DN_HEADS // heads
    nblk = seq // tb
    nchunks = tb // DN_CHUNK
    tokens = batch * seq
    dh = DN_HEAD_DIM

    def col(part):
        return pl.BlockSpec((tb, wblk), lambda b, h, t: (b * nblk + t, part * nhg + h))

    def cw(part):
        return pl.BlockSpec((DN_CONV, wblk), lambda b, h, t: (0, part * nhg + h))

    kern = functools.partial(_gdn_kernel, heads=heads)
    return pl.pallas_call(
        kern,
        grid=(batch, nhg, nblk),
        in_specs=[col(0), col(1), col(2), col(3),
                  pl.BlockSpec((tb, LANES), lambda b, h, t: (b * nblk + t, 0)),
                  cw(0), cw(1), cw(2),
                  pl.BlockSpec((1, dh), lambda b, h, t: (0, 0))],
        out_specs=pl.BlockSpec((tb, wblk), lambda b, h, t: (b * nblk + t, h)),
        out_shape=jax.ShapeDtypeStruct((tokens, DN_WIDTH), BF16),
        scratch_shapes=[pltpu.VMEM((3, tb + CARRY, wblk), F32),
                        pltpu.VMEM((3, heads, nchunks, DN_CHUNK, dh), F32),
                        pltpu.VMEM((2, heads, nchunks, DN_CHUNK, dh), F32),
                        pltpu.VMEM((heads, dh, dh), F32),
                        pltpu.VMEM((heads, nchunks, DN_CHUNK, dh), F32),
                        pltpu.VMEM((heads, nchunks, 2 * DN_CHUNK, dh), BF16),
                        pltpu.VMEM((heads, nchunks, DN_CHUNK, DN_CHUNK), BF16),
                        pltpu.VMEM((heads, nchunks, DN_CHUNK, dh), BF16),
                        pltpu.VMEM((heads, nchunks, CARRY, dh), F32)],
        compiler_params=_params("parallel", "parallel", "arbitrary"),
        name="gdn",
    )(qkvz, qkvz, qkvz, qkvz, gb, conv_w, conv_w, conv_w, dn_norm_w.reshape(1, dh).astype(F32))


ATT_SPAN = 2048
ATT_HEADS_PER_STEP = 2
ATT_WIN = 128


def _attn_kernel(*refs):
    qkv = refs[:3 * DA_GROUPS]
    y_ref = refs[3 * DA_GROUPS]
    ext = refs[3 * DA_GROUPS + 1:3 * DA_GROUPS + 1 + 2 * DA_GROUPS]
    acc_ref, m_ref, l_ref = refs[3 * DA_GROUPS + 1 + 2 * DA_GROUPS:]
    span = pl.program_id(2)
    win = ATT_WIN
    dh = DA_HEAD_DIM
    hp = ATT_HEADS_PER_STEP

    row = lax.broadcasted_iota(jnp.int32, (win, 2 * win), 0)
    colm = lax.broadcasted_iota(jnp.int32, (win, 2 * win), 1)
    band = (colm >= row) & (colm <= row + win)

    for gi, (window, dil) in enumerate(DA_PATTERNS):
        q_ref, k_ref, v_ref = qkv[3 * gi:3 * gi + 3]
        kext, vext = ext[2 * gi:2 * gi + 2]
        n_rows = ATT_SPAN // dil
        nblk = n_rows // win

        @pl.when(span == 0)
        def _():
            kext[:, 0:win, :] = jnp.zeros((dil, win, hp * dh), kext.dtype)
            vext[:, 0:win, :] = jnp.zeros((dil, win, hp * dh), vext.dtype)

        kext[:, win:win + n_rows, :] = k_ref[...]
        vext[:, win:win + n_rows, :] = v_ref[...]

        def unit_pair(it, carry, gi=gi, dil=dil, nblk=nblk, q_ref=q_ref, kext=kext, vext=vext):
            work = []
            for sub in range(2):
                u = 2 * it + sub
                r = u // nblk
                jb = u % nblk
                r0 = pl.multiple_of(jb * win, win)
                mask = band & ((jb > 0) | (span > 0) | (colm >= win))
                if dil > 1:
                    nat = pl.ds(r + jb * (win * dil), win, stride=dil)
                else:
                    nat = pl.ds(r0, win)
                for hh in range(hp):
                    cs = slice(hh * dh, (hh + 1) * dh)
                    prev = None
                    if gi > 0:
                        prev = (acc_ref[hh, nat, :], m_ref[hh, nat, :], l_ref[hh, nat, :])
                    work.append((hh, nat, mask, q_ref[r, pl.ds(r0, win), cs], kext[r, pl.ds(r0, 2 * win), cs],
                                 vext[r, pl.ds(r0, 2 * win), cs], prev))
            results = []
            for hh, nat, mask, qb, kw, vw, prev in work:
                s = jnp.where(mask, _dot_nt(qb, kw), -jnp.inf)
                mx = jnp.max(s, axis=-1, keepdims=True)
                if prev is None:
                    m_new = jnp.broadcast_to(mx, (win, dh))
                    p = jnp.exp(s - mx)
                    l_new = jnp.broadcast_to(jnp.sum(p, axis=-1, keepdims=True), (win, dh))
                    acc = _dot(p.astype(BF16), vw)
                else:
                    acc_prev, m_prev, l_prev = prev
                    m_new = jnp.maximum(m_prev, mx)
                    alpha = jnp.exp(m_prev - m_new)
                    p = jnp.exp(s - m_new[:, 0:1])
                    l_new = l_prev * alpha + jnp.sum(p, axis=-1, keepdims=True)
                    acc = acc_prev * alpha + _dot(p.astype(BF16), vw)
                results.append((hh, nat, acc, m_new, l_new))
            for hh, nat, acc, m_new, l_new in results:
                acc_ref[hh, nat, :] = acc
                m_ref[hh, nat, :] = m_new
                l_ref[hh, nat, :] = l_new
            return carry

        lax.fori_loop(0, dil * nblk // 2, unit_pair, 0)

        kext[:, 0:win, :] = kext[:, n_rows:n_rows + win, :]
        vext[:, 0:win, :] = vext[:, n_rows:n_rows + win, :]

    for hh in range(hp):
        y_ref[:, hh * dh:(hh + 1) * dh] = (acc_ref[hh] / l_ref[hh]).astype(y_ref.dtype)


def _attention(streams, batch, seq):
    hp = ATT_HEADS_PER_STEP
    wblk = hp * DA_HEAD_DIM
    nhp = DA_HEADS_PER_GROUP // hp
    nspan = seq // ATT_SPAN
    in_specs, args, scratch = [], [], []
    for gi, (window, dil) in enumerate(DA_PATTERNS):
        n_rows = ATT_SPAN // dil
        for part in range(3):
            in_specs.append(pl.BlockSpec((None, dil, n_rows, wblk),
                                         lambda b, h, s, part=part: (b, 0, s, part * nhp + h)))
            args.append(streams[gi])
        scratch += [pltpu.VMEM((dil, n_rows + ATT_WIN, wblk), BF16)] * 2
    scratch += [pltpu.VMEM((hp, ATT_SPAN, DA_HEAD_DIM), F32)] * 3
    return pl.pallas_call(
        _attn_kernel,
        grid=(batch, nhp, nspan),
        in_specs=in_specs,
        out_specs=pl.BlockSpec((ATT_SPAN, wblk), lambda b, h, s: (b * nspan + s, h)),
        out_shape=jax.ShapeDtypeStruct((batch * seq, DA_WIDTH), BF16),
        scratch_shapes=scratch,
        compiler_params=_params("parallel", "parallel", "arbitrary"),
        name="dilated_attn",
    )(*args)


def _merge_kernel(ya_ref, yb_ref, wa_ref, wb_ref, ga_ref, gb_ref, o_ref):
    a = _dot(ya_ref[...], wa_ref[...])
    b = _dot(yb_ref[...], wb_ref[...])
    o_ref[...] = (_sigmoid(ga_ref[...].astype(F32)) * a + _sigmoid(gb_ref[...].astype(F32)) * b).astype(o_ref.dtype)


def _merge(ya, yb, wa, wb, gates, tm=1024, tn=1024):
    m = ya.shape[0]
    n = wa.shape[1]
    nt = n // tn
    return pl.pallas_call(
        _merge_kernel,
        grid=(m // tm, nt),
        in_specs=[pl.BlockSpec((tm, ya.shape[1]), lambda i, j: (i, 0)),
                  pl.BlockSpec((tm, yb.shape[1]), lambda i, j: (i, 0)),
                  pl.BlockSpec((wa.shape[0], tn), lambda i, j: (0, j)),
                  pl.BlockSpec((wb.shape[0], tn), lambda i, j: (0, j)),
                  pl.BlockSpec((tm, tn), lambda i, j: (i, j)),
                  pl.BlockSpec((tm, tn), lambda i, j: (i, nt + j))],
        out_specs=pl.BlockSpec((tm, tn), lambda i, j: (i, j)),
        out_shape=jax.ShapeDtypeStruct((m, n), BF16),
        compiler_params=_params("parallel", "arbitrary"),
        name="merge",
    )(ya, yb, wa, wb, gates, gates)


def _layer_norm(y, g, b):
    mu = jnp.mean(y, axis=-1, keepdims=True)
    yc = y - mu
    var = jnp.mean(yc * yc, axis=-1, keepdims=True)
    return yc * lax.rsqrt(var + LN_EPS) * g + b


def _outproj_kernel(m_ref, w_ref, x_ref, g_ref, b_ref, rw_ref, rb_ref, xo_ref, xb_ref, comb_ref):
    y = DEEPNORM_ALPHA * x_ref[...] + _dot(m_ref[...], w_ref[...])
    x1 = _layer_norm(y, g_ref[...], b_ref[...])
    xo_ref[...] = x1
    xb_ref[...] = x1.astype(xb_ref.dtype)
    logits = _dot_f32(x1, rw_ref[...]) + rb_ref[...]
    lane = lax.broadcasted_iota(jnp.int32, logits.shape, 1)
    work = jnp.where(lane < N_EXPERTS, logits, -jnp.inf)
    top = jnp.max(work, axis=-1, keepdims=True)
    chosen = jnp.zeros(logits.shape, dtype=jnp.bool_)
    for _ in range(TOP_K):
        mx = jnp.max(work, axis=-1, keepdims=True)
        first_idx = jnp.min(jnp.where(work == mx, lane, LANES), axis=-1, keepdims=True)
        sel = lane == first_idx
        chosen = chosen | sel
        work = jnp.where(sel, -jnp.inf, work)
    e = jnp.where(chosen, jnp.exp(logits - top), 0.0)
    comb_ref[...] = e / jnp.sum(e, axis=-1, keepdims=True)


def _outproj_ln_router(merged, w_out, x, ln_g, ln_b, router_w, router_b, tm=512):
    m, d = x.shape
    rw = jnp.pad(router_w.astype(F32), ((0, 0), (0, LANES - N_EXPERTS)))
    rb = jnp.pad(router_b.astype(F32), (0, LANES - N_EXPERTS)).reshape(1, LANES)
    row = lambda i: (i, 0)
    fixed = lambda i: (0, 0)
    return pl.pallas_call(
        _outproj_kernel,
        grid=(m // tm,),
        in_specs=[pl.BlockSpec((tm, d), row),
                  pl.BlockSpec((d, d), fixed),
                  pl.BlockSpec((tm, d), row),
                  pl.BlockSpec((1, d), fixed),
                  pl.BlockSpec((1, d), fixed),
                  pl.BlockSpec((d, LANES), fixed),
                  pl.BlockSpec((1, LANES), fixed)],
        out_specs=[pl.BlockSpec((tm, d), row), pl.BlockSpec((tm, d), row), pl.BlockSpec((tm, LANES), row)],
        out_shape=[jax.ShapeDtypeStruct((m, d), F32), jax.ShapeDtypeStruct((m, d), BF16),
                   jax.ShapeDtypeStruct((m, LANES), F32)],
        compiler_params=_params("parallel"),
        name="outproj_ln_router",
    )(merged, w_out, x, ln_g.reshape(1, d).astype(F32), ln_b.reshape(1, d).astype(F32), rw, rb)


def _moe_kernel(xb_ref, x_ref, comb_ref, wgu_ref, bgu_ref, wd_ref, bd_ref, g_ref, b_ref,
                xo_ref, xbo_ref, acc_ref):
    e = pl.program_id(1)

    @pl.when(e == 0)
    def _():
        acc_ref[...] = jnp.zeros_like(acc_ref)

    gu = _dot(xb_ref[...], wgu_ref[...]) + bgu_ref[...]
    gate = jnp.minimum(gu[:, :D_EXPERT], SWIGLU_LIMIT)
    up = jnp.clip(gu[:, D_EXPERT:], -SWIGLU_LIMIT, SWIGLU_LIMIT)
    glu = gate * _sigmoid(gate * SWIGLU_ALPHA)
    y = _dot(((up + 1.0) * glu).astype(BF16), wd_ref[...]) + bd_ref[...]
    comb = comb_ref[...]
    lane = lax.broadcasted_iota(jnp.int32, comb.shape, 1)
    ce = jnp.sum(jnp.where(lane == e, comb, 0.0), axis=-1, keepdims=True)
    acc_ref[...] += ce * y

    @pl.when(e == pl.num_programs(1) - 1)
    def _():
        x2 = _layer_norm(DEEPNORM_ALPHA * x_ref[...] + acc_ref[...], g_ref[...], b_ref[...])
        xo_ref[...] = x2
        xbo_ref[...] = x2.astype(xbo_ref.dtype)


def _moe_ln(xb, x, comb, wgu, bgu, wd, bd, ln_g, ln_b, tm=512):
    m, d = x.shape
    ne = wgu.shape[0]
    row = lambda i, e: (i, 0)
    fixed = lambda i, e: (0, 0)
    return pl.pallas_call(
        _moe_kernel,
        grid=(m // tm, ne),
        in_specs=[pl.BlockSpec((tm, d), row),
                  pl.BlockSpec((tm, d), row),
                  pl.BlockSpec((tm, LANES), row),
                  pl.BlockSpec((None, d, 2 * D_EXPERT), lambda i, e: (e, 0, 0)),
                  pl.BlockSpec((None, 1, 2 * D_EXPERT), lambda i, e: (e, 0, 0)),
                  pl.BlockSpec((None, D_EXPERT, d), lambda i, e: (e, 0, 0)),
                  pl.BlockSpec((None, 1, d), lambda i, e: (e, 0, 0)),
                  pl.BlockSpec((1, d), fixed),
                  pl.BlockSpec((1, d), fixed)],
        out_specs=[pl.BlockSpec((tm, d), row), pl.BlockSpec((tm, d), row)],
        out_shape=[jax.ShapeDtypeStruct((m, d), F32), jax.ShapeDtypeStruct((m, d), BF16)],
        scratch_shapes=[pltpu.VMEM((tm, d), F32)],
        compiler_params=_params("parallel", "arbitrary"),
        name="moe_ln",
    )(xb, x, comb, wgu, bgu.reshape(ne, 1, -1).astype(F32), wd, bd.reshape(ne, 1, -1).astype(F32),
      ln_g.reshape(1, d).astype(F32), ln_b.reshape(1, d).astype(F32))


def _rope_tables(seq):
    half = DA_HEAD_DIM // 2
    inv_freq = ROPE_THETA ** (-jnp.arange(half, dtype=F32) / half)
    ang = jnp.arange(seq, dtype=F32)[:, None] * inv_freq[None, :]
    cos = jnp.cos(ang)
    sin = jnp.sin(ang)
    return jnp.concatenate([cos, cos], axis=-1), jnp.concatenate([-sin, sin], axis=-1)


def _group_qkv_weights(w_qkvb, gi):
    part_w = DA_GROUPS * DA_WIDTH
    return jnp.concatenate([w_qkvb[:, p * part_w + gi * DA_WIDTH:p * part_w + (gi + 1) * DA_WIDTH]
                            for p in range(3)], axis=1)


def _layer(x, xb, batch, seq, cosf, sinf, w_in, conv_w, a_log, dt_bias, dn_norm_w, w_branch_a, w_branch_b,
           w_out, ln1_g, ln1_b, router_w, router_b, w_gate_up, b_gate_up, w_down, b_down, ln2_g, ln2_b):
    w_qkvz = w_in[:, OFF_QKVZ:OFF_AB].astype(BF16)
    w_ab = w_in[:, OFF_AB:OFF_QKVB]
    w_qkvb = w_in[:, OFF_QKVB:OFF_GATES].astype(BF16)
    w_gates = w_in[:, OFF_GATES:].astype(BF16)

    qkvz = _matmul(xb, w_qkvz, BF16)
    gb = _dn_gates(x, w_ab, a_log, dt_bias)
    streams = [_matmul_streams(xb, _group_qkv_weights(w_qkvb, gi), cosf, sinf, batch, seq, dil)
               for gi, (_, dil) in enumerate(DA_PATTERNS)]
    gates = _matmul(xb, w_gates, BF16)

    y_a = _gdn(qkvz, gb, conv_w.astype(F32), dn_norm_w, batch, seq)
    y_b = _attention(streams, batch, seq)

    merged = _merge(y_a, y_b, w_branch_a.astype(BF16), w_branch_b.astype(BF16), gates)
    x1, x1b, comb = _outproj_ln_router(merged, w_out.astype(BF16), x, ln1_g, ln1_b, router_w, router_b)
    return _moe_ln(x1b, x1, comb, w_gate_up.astype(BF16), b_gate_up, w_down.astype(BF16), b_down, ln2_g, ln2_b)


@jax.jit
def kernel(x, w_in, conv_w, a_log, dt_bias, dn_norm_w, w_branch_a, w_branch_b, w_out, ln1_g, ln1_b,
           router_w, router_b, w_gate_up, b_gate_up, w_down, b_down, ln2_g, ln2_b):
    batch, seq, d = x.shape
    cosf, sinf = _rope_tables(seq)
    xf = x.reshape(batch * seq, d)
    xb = xf.astype(BF16)
    for l in range(w_in.shape[0]):
        xf, xb = _layer(xf, xb, batch, seq, cosf, sinf, w_in[l], conv_w[l], a_log[l], dt_bias[l], dn_norm_w[l],
                        w_branch_a[l], w_branch_b[l], w_out[l], ln1_g[l], ln1_b[l], router_w[l], router_b[l],
                        w_gate_up[l], b_gate_up[l], w_down[l], b_down[l], ln2_g[l], ln2_b[l])
    return xf.reshape(batch, seq, d)
```

```python
import functools

import jax
import jax.numpy as jnp
from jax import lax
from jax.experimental import pallas as pl
from jax.experimental.pallas import tpu as pltpu

F32 = jnp.float32
BF16 = jnp.bfloat16
HIGHEST = lax.Precision.HIGHEST

D_MODEL = 2048
DEPTH = 4
DN_HEADS = 16
DN_HEAD_DIM = 128
DN_WIDTH = DN_HEADS * DN_HEAD_DIM
DN_CONV = 4
DN_CHUNK = 64
DA_PATTERNS = ((128, 1), (512, 4), (2048, 16))
DA_GROUPS = len(DA_PATTERNS)
DA_HEADS_PER_GROUP = 8
DA_HEAD_DIM = 128
DA_WIDTH = DA_HEADS_PER_GROUP * DA_HEAD_DIM
ROPE_THETA = 10000.0
N_EXPERTS = 32
TOP_K = 4
D_EXPERT = 512
SWIGLU_LIMIT = 7.0
SWIGLU_ALPHA = 1.702
DEEPNORM_ALPHA = (2 * DEPTH) ** 0.25
LN_EPS = 1e-5
NORM_EPS = 1e-6

LANES = 128
VMEM_LIMIT = 56 * 1024 * 1024

OFF_QKVZ = 0
OFF_AB = 4 * DN_WIDTH
OFF_QKVB = OFF_AB + 2 * DN_HEADS
OFF_GATES = OFF_QKVB + 3 * DA_GROUPS * DA_WIDTH
IN_WIDTH = OFF_GATES + 2 * D_MODEL


def _params(*sem):
    return pltpu.CompilerParams(dimension_semantics=sem, vmem_limit_bytes=VMEM_LIMIT)


def _sigmoid(v):
    return 1.0 / (1.0 + jnp.exp(-v))


def _dot(a, b):
    return jnp.dot(a, b, preferred_element_type=F32)


def _dot_f32(a, b):
    return jnp.dot(a, b, preferred_element_type=F32, precision=HIGHEST)


def _dot_nt(a, b):
    return lax.dot_general(a, b, (((1,), (1,)), ((), ())), preferred_element_type=F32)


def _dot_tn(a, b):
    return lax.dot_general(a, b, (((0,), (0,)), ((), ())), preferred_element_type=F32)


def _split_bf16(a):
    hi = a.astype(BF16)
    return hi, (a - hi.astype(F32)).astype(BF16)


def _dot_split(a, b):
    ah, al = a
    bh, bl = b
    return _dot(ah, bh) + (_dot(ah, bl) + _dot(al, bh))


def _mm_kernel(x_ref, w_ref, o_ref):
    o_ref[...] = _dot(x_ref[...], w_ref[...]).astype(o_ref.dtype)


def _matmul(x, w, out_dtype, tm=1024, tn=1024):
    m, k = x.shape
    n = w.shape[1]
    return pl.pallas_call(
        _mm_kernel,
        grid=(m // tm, n // tn),
        in_specs=[pl.BlockSpec((tm, k), lambda i, j: (i, 0)),
                  pl.BlockSpec((k, tn), lambda i, j: (0, j))],
        out_specs=pl.BlockSpec((tm, tn), lambda i, j: (i, j)),
        out_shape=jax.ShapeDtypeStruct((m, n), out_dtype),
        compiler_params=_params("parallel", "arbitrary"),
        name="matmul",
    )(x, w)


def _mm_stream_kernel(x_ref, w_ref, cos_ref, sin_ref, o_ref, acc_ref, *, dil, q_scale):
    j = pl.program_id(1)
    tm = x_ref.shape[0]
    rows_per = tm // dil
    half = DA_HEAD_DIM // 2
    acc = _dot(x_ref[...], w_ref[...])
    for hh in range(DA_HEADS_PER_GROUP):
        acc_ref[hh] = acc[:, hh * DA_HEAD_DIM:(hh + 1) * DA_HEAD_DIM]

    def rows(r):
        return pl.ds(r, rows_per, stride=dil) if dil > 1 else slice(None)

    @pl.when(j == 2)
    def _():
        for r in range(dil):
            for hh in range(DA_HEADS_PER_GROUP):
                cs = slice(hh * DA_HEAD_DIM, (hh + 1) * DA_HEAD_DIM)
                o_ref[r, :, cs] = acc_ref[hh, rows(r), :].astype(o_ref.dtype)

    @pl.when(j < 2)
    def _():
        scale = jnp.where(j == 0, q_scale, 1.0).astype(F32)
        for r in range(dil):
            cosf = cos_ref[rows(r), :] * scale
            sinf = sin_ref[rows(r), :] * scale
            for hh in range(DA_HEADS_PER_GROUP):
                cs = slice(hh * DA_HEAD_DIM, (hh + 1) * DA_HEAD_DIM)
                t = acc_ref[hh, rows(r), :]
                o_ref[r, :, cs] = (t * cosf + pltpu.roll(t, half, 1) * sinf).astype(o_ref.dtype)


def _matmul_streams(x, w, cosf, sinf, batch, seq, dil, tm=1024):
    m, k = x.shape
    nseq = seq // tm
    n = seq // dil
    kern = functools.partial(_mm_stream_kernel, dil=dil, q_scale=DA_HEAD_DIM ** -0.5)
    return pl.pallas_call(
        kern,
        grid=(m // tm, 3),
        in_specs=[pl.BlockSpec((tm, k), lambda i, j: (i, 0)),
                  pl.BlockSpec((k, DA_WIDTH), lambda i, j: (0, j)),
                  pl.BlockSpec((tm, DA_HEAD_DIM), lambda i, j: (i % nseq, 0)),
                  pl.BlockSpec((tm, DA_HEAD_DIM), lambda i, j: (i % nseq, 0))],
        out_specs=pl.BlockSpec((None, dil, tm // dil, DA_WIDTH), lambda i, j: (i // nseq, 0, i % nseq, j)),
        out_shape=jax.ShapeDtypeStruct((batch, dil, n, 3 * DA_WIDTH), BF16),
        scratch_shapes=[pltpu.VMEM((DA_HEADS_PER_GROUP, tm, DA_HEAD_DIM), F32)],
        compiler_params=_params("parallel", "arbitrary"),
        name=f"qkv_streams_d{dil}",
    )(x, w, cosf, sinf)


def _gates_kernel(x_ref, w_ref, a_ref, dt_ref, o_ref):
    lin = _dot_f32(x_ref[...], w_ref[...])
    xa = lin + dt_ref[...]
    softplus = jnp.maximum(xa, 0.0) + jnp.log(1.0 + jnp.exp(-jnp.abs(xa)))
    g = -jnp.exp(a_ref[...]) * softplus
    beta = _sigmoid(lin)
    c = DN_CHUNK
    ri = lax.broadcasted_iota(jnp.int32, (c, c), 0)
    ci = lax.broadcasted_iota(jnp.int32, (c, c), 1)
    tri = (ri >= ci).astype(F32)
    gcum = jnp.concatenate([_dot_f32(tri, g[i * c:(i + 1) * c]) for i in range(lin.shape[0] // c)], axis=0)
    lane = lax.broadcasted_iota(jnp.int32, lin.shape, 1)
    o_ref[...] = jnp.where(lane < DN_HEADS, gcum, beta)


def _dn_gates(x, w_ab, a_log, dt_bias, tm=512):
    m, k = x.shape
    pad = LANES - 2 * DN_HEADS
    w = jnp.pad(w_ab, ((0, 0), (0, pad)))
    a = jnp.pad(a_log.astype(F32), (0, LANES - DN_HEADS)).reshape(1, LANES)
    dt = jnp.pad(dt_bias.astype(F32), (0, LANES - DN_HEADS)).reshape(1, LANES)
    return pl.pallas_call(
        _gates_kernel,
        grid=(m // tm,),
        in_specs=[pl.BlockSpec((tm, k), lambda i: (i, 0)),
                  pl.BlockSpec((k, LANES), lambda i: (0, 0)),
                  pl.BlockSpec((1, LANES), lambda i: (0, 0)),
                  pl.BlockSpec((1, LANES), lambda i: (0, 0))],
        out_specs=pl.BlockSpec((tm, LANES), lambda i: (i, 0)),
        out_shape=jax.ShapeDtypeStruct((m, LANES), F32),
        compiler_params=_params("parallel"),
        name="dn_gates",
    )(x, w, a, dt)


GDN_BLOCK = 512
GDN_HEADS_PER_STEP = 8
CARRY = 8


def _bdot(a, b):
    return lax.dot_general(a, b, (((2,), (1,)), ((0,), (0,))), preferred_element_type=F32)


def _bdot_nt(a, b):
    return lax.dot_general(a, b, (((2,), (2,)), ((0,), (0,))), preferred_element_type=F32)


def _bdot_split(a, b):
    ah, al = a
    bh, bl = b
    return _bdot(ah, bh) + (_bdot(ah, bl) + _bdot(al, bh))


def _unit_lower_inverse(lmat):
    c = lmat.shape[-1]
    ri = lax.broadcasted_iota(jnp.int32, lmat.shape, 1)
    ci = lax.broadcasted_iota(jnp.int32, lmat.shape, 2)
    eye = (ri == ci).astype(F32)
    s = 1
    minv = eye
    while s < c:
        shift = s.bit_length()
        off = ((ri >> shift) == (ci >> shift)) & ((ri & s) != 0) & ((ci & s) == 0)
        boff = jnp.where(off, lmat, 0.0)
        if s == 1:
            minv = eye - boff
        else:
            ms = _split_bf16(minv)
            minv = minv - _bdot_split(_split_bf16(_bdot_split(ms, _split_bf16(boff))), ms)
        s *= 2
    return minv


def _gdn_kernel(q_ref, k_ref, v_ref, z_ref, gb_ref, cwq_ref, cwk_ref, cwv_ref, nw_ref, o_ref,
                ext_ref, qkv_ref, gbc_ref, state_ref, u_ref, wq_ref, intra_ref, kd_ref, egl_ref, *, heads):
    hg = pl.program_id(1)
    t = pl.program_id(2)
    tb = q_ref.shape[0]
    dh = DN_HEAD_DIM
    c = DN_CHUNK
    nchunks = tb // c

    @pl.when(t == 0)
    def _():
        state_ref[...] = jnp.zeros_like(state_ref)
        ext_ref[:, 0:CARRY, :] = jnp.zeros((3, CARRY, ext_ref.shape[2]), F32)

    for idx, (ref, cw_ref) in enumerate(((q_ref, cwq_ref), (k_ref, cwk_ref), (v_ref, cwv_ref))):
        ext_ref[idx, CARRY:CARRY + tb, :] = ref[...].astype(F32)
        acc = ext_ref[idx, CARRY:CARRY + tb, :] * cw_ref[DN_CONV - 1:DN_CONV, :]
        for back in range(1, DN_CONV):
            acc = acc + (ext_ref[idx, CARRY - back:CARRY - back + tb, :]
                         * cw_ref[DN_CONV - 1 - back:DN_CONV - back, :])
        ext_ref[idx, 0:CARRY, :] = ext_ref[idx, tb:tb + CARRY, :]
        act = acc * _sigmoid(acc)
        for g in range(heads):
            a = act[:, g * dh:(g + 1) * dh]
            if idx < 2:
                a = a * lax.rsqrt(jnp.sum(a * a, axis=-1, keepdims=True) + NORM_EPS)
            if idx == 0:
                a = a * (dh ** -0.5)
            qkv_ref[idx, g] = a.reshape(nchunks, c, dh)

    gb = gb_ref[...]
    lane = lax.broadcasted_iota(jnp.int32, gb.shape, 1)
    for g in range(heads):
        h = hg * heads + g
        gcol = jnp.sum(jnp.where(lane == h, gb, 0.0), axis=-1, keepdims=True)
        bcol = jnp.sum(jnp.where(lane == h + DN_HEADS, gb, 0.0), axis=-1, keepdims=True)
        gbc_ref[0, g] = jnp.broadcast_to(gcol, (tb, dh)).reshape(nchunks, c, dh)
        gbc_ref[1, g] = jnp.broadcast_to(bcol, (tb, dh)).reshape(nchunks, c, dh)

    ri = lax.broadcasted_iota(jnp.int32, (nchunks, c, c), 1)
    ci = lax.broadcasted_iota(jnp.int32, (nchunks, c, c), 2)

    def head_body(g, carry):
        q = qkv_ref[0, g]
        k = qkv_ref[1, g]
        v = qkv_ref[2, g]
        gc = gbc_ref[0, g]
        beta = gbc_ref[1, g]
        gsq = gc[:, :, :c]
        diff = gsq - jnp.swapaxes(gsq, 1, 2)
        decay = jnp.where(ri >= ci, jnp.exp(jnp.minimum(diff, 0.0)), 0.0)
        kb = k * beta
        both = _bdot_nt(jnp.concatenate([q, kb], axis=1).astype(BF16), k.astype(BF16))
        lower = jnp.where(ri > ci, both[:, c:] * decay, 0.0)
        tinv = _unit_lower_inverse(lower)
        eg = jnp.exp(gc)
        rhs = jnp.concatenate([v * beta, kb * eg], axis=2).astype(BF16)
        uw = _bdot(tinv.astype(BF16), rhs)
        u_ref[g] = uw[:, :, :dh]
        wq_ref[g] = jnp.concatenate([uw[:, :, dh:], q * eg], axis=1).astype(BF16)
        intra_ref[g] = (both[:, :c] * decay).astype(BF16)
        g_last = gc[:, c - 1:c, :]
        kd_ref[g] = (k * jnp.exp(g_last - gc)).astype(BF16)
        egl_ref[g] = jnp.broadcast_to(jnp.exp(g_last), (nchunks, CARRY, dh))
        return carry

    lax.fori_loop(0, heads, head_body, 0)

    nw = nw_ref[...]

    def chunk_body(ic, carry):
        rows = pl.ds(pl.multiple_of(ic * c, c), c)
        states = [state_ref[g] for g in range(heads)]
        outs = []
        for g in range(heads):
            ws_qs = _dot(wq_ref[g, ic], states[g].astype(BF16))
            vnb = (u_ref[g, ic] - ws_qs[:c]).astype(BF16)
            o = ws_qs[c:] + _dot(intra_ref[g, ic], vnb)
            states[g] = states[g] * egl_ref[g, ic, 0:1, :] + _dot_tn(kd_ref[g, ic], vnb)
            o = o * lax.rsqrt(jnp.mean(o * o, axis=-1, keepdims=True) + NORM_EPS) * nw
            zz = z_ref[rows, g * dh:(g + 1) * dh].astype(F32)
            outs.append((o * (zz * _sigmoid(zz))).astype(o_ref.dtype))
        for g in range(heads):
            state_ref[g] = states[g]
            o_ref[rows, g * dh:(g + 1) * dh] = outs[g]
        return carry

    lax.fori_loop(0, nchunks, chunk_body, 0)


def _gdn(qkvz, gb, conv_w, dn_norm_w, batch, seq):
    tb = GDN_BLOCK
    heads = GDN_HEADS_PER_STEP
    wblk = heads * DN_HEAD_DIM
    nhg = DN_HEADS // heads
    nblk = seq // tb
    nchunks = tb // DN_CHUNK
    tokens = batch * seq
    dh = DN_HEAD_DIM

    def col(part):
        return pl.BlockSpec((tb, wblk), lambda b, h, t: (b * nblk + t, part * nhg + h))

    def cw(part):
        return pl.BlockSpec((DN_CONV, wblk), lambda b, h, t: (0, part * nhg + h))

    kern = functools.partial(_gdn_kernel, heads=heads)
    return pl.pallas_call(
        kern,
        grid=(batch, nhg, nblk),
        in_specs=[col(0), col(1), col(2), col(3),
                  pl.BlockSpec((tb, LANES), lambda b, h, t: (b * nblk + t, 0)),
                  cw(0), cw(1), cw(2),
                  pl.BlockSpec((1, dh), lambda b, h, t: (0, 0))],
        out_specs=pl.BlockSpec((tb, wblk), lambda b, h, t: (b * nblk + t, h)),
        out_shape=jax.ShapeDtypeStruct((tokens, DN_WIDTH), BF16),
        scratch_shapes=[pltpu.VMEM((3, tb + CARRY, wblk), F32),
                        pltpu.VMEM((3, heads, nchunks, DN_CHUNK, dh), F32),
                        pltpu.VMEM((2, heads, nchunks, DN_CHUNK, dh), F32),
                        pltpu.VMEM((heads, dh, dh), F32),
                        pltpu.VMEM((heads, nchunks, DN_CHUNK, dh), F32),
                        pltpu.VMEM((heads, nchunks, 2 * DN_CHUNK, dh), BF16),
                        pltpu.VMEM((heads, nchunks, DN_CHUNK, DN_CHUNK), BF16),
                        pltpu.VMEM((heads, nchunks, DN_CHUNK, dh), BF16),
                        pltpu.VMEM((heads, nchunks, CARRY, dh), F32)],
        compiler_params=_params("parallel", "parallel", "arbitrary"),
        name="gdn",
    )(qkvz, qkvz, qkvz, qkvz, gb, conv_w, conv_w, conv_w, dn_norm_w.reshape(1, dh).astype(F32))


ATT_SPAN = 2048
ATT_HEADS_PER_STEP = 2
ATT_WIN = 128


def _attn_kernel(*refs):
    qkv = refs[:3 * DA_GROUPS]
    y_ref = refs[3 * DA_GROUPS]
    ext = refs[3 * DA_GROUPS + 1:3 * DA_GROUPS + 1 + 2 * DA_GROUPS]
    acc_ref, m_ref, l_ref = refs[3 * DA_GROUPS + 1 + 2 * DA_GROUPS:]
    span = pl.program_id(2)
    win = ATT_WIN
    dh = DA_HEAD_DIM
    hp = ATT_HEADS_PER_STEP

    row = lax.broadcasted_iota(jnp.int32, (win, 2 * win), 0)
    colm = lax.broadcasted_iota(jnp.int32, (win, 2 * win), 1)
    band = (colm >= row) & (colm <= row + win)

    for gi, (window, dil) in enumerate(DA_PATTERNS):
        q_ref, k_ref, v_ref = qkv[3 * gi:3 * gi + 3]
        kext, vext = ext[2 * gi:2 * gi + 2]
        n_rows = ATT_SPAN // dil
        nblk = n_rows // win

        @pl.when(span == 0)
        def _():
            kext[:, 0:win, :] = jnp.zeros((dil, win, hp * dh), kext.dtype)
            vext[:, 0:win, :] = jnp.zeros((dil, win, hp * dh), vext.dtype)

        kext[:, win:win + n_rows, :] = k_ref[...]
        vext[:, win:win + n_rows, :] = v_ref[...]

        def unit_pair(it, carry, gi=gi, dil=dil, nblk=nblk, q_ref=q_ref, kext=kext, vext=vext):
            work = []
            for sub in range(2):
                u = 2 * it + sub
                r = u // nblk
                jb = u % nblk
                r0 = pl.multiple_of(jb * win, win)
                mask = band & ((jb > 0) | (span > 0) | (colm >= win))
                if dil > 1:
                    nat = pl.ds(r + jb * (win * dil), win, stride=dil)
                else:
                    nat = pl.ds(r0, win)
                for hh in range(hp):
                    cs = slice(hh * dh, (hh + 1) * dh)
                    prev = None
                    if gi > 0:
                        prev = (acc_ref[hh, nat, :], m_ref[hh, nat, :], l_ref[hh, nat, :])
                    work.append((hh, nat, mask, q_ref[r, pl.ds(r0, win), cs], kext[r, pl.ds(r0, 2 * win), cs],
                                 vext[r, pl.ds(r0, 2 * win), cs], prev))
            results = []
            for hh, nat, mask, qb, kw, vw, prev in work:
                s = jnp.where(mask, _dot_nt(qb, kw), -jnp.inf)
                mx = jnp.max(s, axis=-1, keepdims=True)
                if prev is None:
                    m_new = jnp.broadcast_to(mx, (win, dh))
                    p = jnp.exp(s - mx)
                    l_new = jnp.broadcast_to(jnp.sum(p, axis=-1, keepdims=True), (win, dh))
                    acc = _dot(p.astype(BF16), vw)
                else:
                    acc_prev, m_prev, l_prev = prev
                    m_new = jnp.maximum(m_prev, mx)
                    alpha = jnp.exp(m_prev - m_new)
                    p = jnp.exp(s - m_new[:, 0:1])
                    l_new = l_prev * alpha + jnp.sum(p, axis=-1, keepdims=True)
                    acc = acc_prev * alpha + _dot(p.astype(BF16), vw)
                results.append((hh, nat, acc, m_new, l_new))
            for hh, nat, acc, m_new, l_new in results:
                acc_ref[hh, nat, :] = acc
                m_ref[hh, nat, :] = m_new
                l_ref[hh, nat, :] = l_new
            return carry

        lax.fori_loop(0, dil * nblk // 2, unit_pair, 0)

        kext[:, 0:win, :] = kext[:, n_rows:n_rows + win, :]
        vext[:, 0:win, :] = vext[:, n_rows:n_rows + win, :]

    for hh in range(hp):
        y_ref[:, hh * dh:(hh + 1) * dh] = (acc_ref[hh] / l_ref[hh]).astype(y_ref.dtype)


def _attention(streams, batch, seq):
    hp = ATT_HEADS_PER_STEP
    wblk = hp * DA_HEAD_DIM
    nhp = DA_HEADS_PER_GROUP // hp
    nspan = seq // ATT_SPAN
    in_specs, args, scratch = [], [], []
    for gi, (window, dil) in enumerate(DA_PATTERNS):
        n_rows = ATT_SPAN // dil
        for part in range(3):
            in_specs.append(pl.BlockSpec((None, dil, n_rows, wblk),
                                         lambda b, h, s, part=part: (b, 0, s, part * nhp + h)))
            args.append(streams[gi])
        scratch += [pltpu.VMEM((dil, n_rows + ATT_WIN, wblk), BF16)] * 2
    scratch += [pltpu.VMEM((hp, ATT_SPAN, DA_HEAD_DIM), F32)] * 3
    return pl.pallas_call(
        _attn_kernel,
        grid=(batch, nhp, nspan),
        in_specs=in_specs,
        out_specs=pl.BlockSpec((ATT_SPAN, wblk), lambda b, h, s: (b * nspan + s, h)),
        out_shape=jax.ShapeDtypeStruct((batch * seq, DA_WIDTH), BF16),
        scratch_shapes=scratch,
        compiler_params=_params("parallel", "parallel", "arbitrary"),
        name="dilated_attn",
    )(*args)


def _merge_kernel(ya_ref, yb_ref, wa_ref, wb_ref, ga_ref, gb_ref, o_ref):
    a = _dot(ya_ref[...], wa_ref[...])
    b = _dot(yb_ref[...], wb_ref[...])
    o_ref[...] = (_sigmoid(ga_ref[...].astype(F32)) * a + _sigmoid(gb_ref[...].astype(F32)) * b).astype(o_ref.dtype)


def _merge(ya, yb, wa, wb, gates, tm=1024, tn=1024):
    m = ya.shape[0]
    n = wa.shape[1]
    nt = n // tn
    return pl.pallas_call(
        _merge_kernel,
        grid=(m // tm, nt),
        in_specs=[pl.BlockSpec((tm, ya.shape[1]), lambda i, j: (i, 0)),
                  pl.BlockSpec((tm, yb.shape[1]), lambda i, j: (i, 0)),
                  pl.BlockSpec((wa.shape[0], tn), lambda i, j: (0, j)),
                  pl.BlockSpec((wb.shape[0], tn), lambda i, j: (0, j)),
                  pl.BlockSpec((tm, tn), lambda i, j: (i, j)),
                  pl.BlockSpec((tm, tn), lambda i, j: (i, nt + j))],
        out_specs=pl.BlockSpec((tm, tn), lambda i, j: (i, j)),
        out_shape=jax.ShapeDtypeStruct((m, n), BF16),
        compiler_params=_params("parallel", "arbitrary"),
        name="merge",
    )(ya, yb, wa, wb, gates, gates)


def _layer_norm(y, g, b):
    mu = jnp.mean(y, axis=-1, keepdims=True)
    yc = y - mu
    var = jnp.mean(yc * yc, axis=-1, keepdims=True)
    return yc * lax.rsqrt(var + LN_EPS) * g + b


def _outproj_kernel(m_ref, w_ref, x_ref, g_ref, b_ref, rw_ref, rb_ref, xo_ref, eid_ref, ew_ref):
    y = DEEPNORM_ALPHA * x_ref[...] + _dot(m_ref[...], w_ref[...])
    x1 = _layer_norm(y, g_ref[...], b_ref[...])
    xo_ref[...] = x1
    logits = _dot_f32(x1, rw_ref[...]) + rb_ref[...]
    lane = lax.broadcasted_iota(jnp.int32, logits.shape, 1)
    work = jnp.where(lane < N_EXPERTS, logits, -jnp.inf)
    top = jnp.max(work, axis=-1, keepdims=True)
    ids = jnp.zeros(logits.shape, jnp.int32)
    vals = jnp.full(logits.shape, -jnp.inf, F32)
    for kk in range(TOP_K):
        mx = jnp.max(work, axis=-1, keepdims=True)
        first_idx = jnp.min(jnp.where(work == mx, lane, LANES), axis=-1, keepdims=True)
        ids = jnp.where(lane == kk, first_idx, ids)
        vals = jnp.where(lane == kk, mx, vals)
        work = jnp.where(lane == first_idx, -jnp.inf, work)
    e = jnp.exp(vals - top)
    eid_ref[...] = ids
    ew_ref[...] = e / jnp.sum(e, axis=-1, keepdims=True)


def _outproj_ln_router(merged, w_out, x, ln_g, ln_b, router_w, router_b, tm=512):
    m, d = x.shape
    rw = jnp.pad(router_w.astype(F32), ((0, 0), (0, LANES - N_EXPERTS)))
    rb = jnp.pad(router_b.astype(F32), (0, LANES - N_EXPERTS)).reshape(1, LANES)
    row = lambda i: (i, 0)
    fixed = lambda i: (0, 0)
    return pl.pallas_call(
        _outproj_kernel,
        grid=(m // tm,),
        in_specs=[pl.BlockSpec((tm, d), row),
                  pl.BlockSpec((d, d), fixed),
                  pl.BlockSpec((tm, d), row),
                  pl.BlockSpec((1, d), fixed),
                  pl.BlockSpec((1, d), fixed),
                  pl.BlockSpec((d, LANES), fixed),
                  pl.BlockSpec((1, LANES), fixed)],
        out_specs=[pl.BlockSpec((tm, d), row), pl.BlockSpec((tm, LANES), row), pl.BlockSpec((tm, LANES), row)],
        out_shape=[jax.ShapeDtypeStruct((m, d), F32), jax.ShapeDtypeStruct((m, LANES), jnp.int32),
                   jax.ShapeDtypeStruct((m, LANES), F32)],
        compiler_params=_params("parallel"),
        name="outproj_ln_router",
    )(merged, w_out, x, ln_g.reshape(1, d).astype(F32), ln_b.reshape(1, d).astype(F32), rw, rb)


MOE_TILE = 512
MOE_TOKENS_PER_STEP = 256
ROW_SUB = D_MODEL // LANES


def _moe_rows(tokens):
    return tokens * TOP_K + N_EXPERTS * MOE_TILE


def _moe_plan(eid, tokens):
    flat = eid[:, :TOP_K].reshape(-1)
    onehot = (flat[:, None] == jnp.arange(N_EXPERTS, dtype=jnp.int32)[None, :]).astype(jnp.int32)
    csum = jnp.cumsum(onehot, axis=0)
    counts = csum[-1]
    pcount = ((counts + MOE_TILE - 1) // MOE_TILE) * MOE_TILE
    pend = jnp.cumsum(pcount)
    start = pend - pcount
    slot = jnp.sum(onehot * (start[None, :] + csum - 1), axis=1)
    ntiles = _moe_rows(tokens) // MOE_TILE
    tile_start = jnp.arange(ntiles, dtype=jnp.int32) * MOE_TILE
    tile_valid = (tile_start < pend[-1]).astype(jnp.int32)
    last_start = jnp.maximum(pend[-1] - MOE_TILE, 0)
    tile_e = jnp.sum((jnp.minimum(tile_start, last_start)[:, None] >= pend[None, :]).astype(jnp.int32), axis=1)
    nstep = tokens // MOE_TOKENS_PER_STEP
    return slot.astype(jnp.int32).reshape(nstep, 1, MOE_TOKENS_PER_STEP * TOP_K), tile_e, tile_valid


def _row(i):
    return pl.ds(pl.multiple_of(i * ROW_SUB, ROW_SUB), ROW_SUB)


def _stage_slots(slot_ref, slot_smem, sem):
    cp = pltpu.make_async_copy(slot_ref.at[0], slot_smem, sem)
    cp.start()
    cp.wait()


def _dispatch_kernel(slot_ref, x_ref, xs_in, xs_out, x3_ref, slot_smem, sem, slot_sem):
    del xs_in
    tm = x_ref.shape[0]
    _stage_slots(slot_ref, slot_smem, slot_sem)
    for j in range(ROW_SUB):
        x3_ref[pl.ds(j, tm, stride=ROW_SUB), :] = x_ref[:, j * LANES:(j + 1) * LANES]

    def issue(t, carry):
        src = x3_ref.at[_row(t)]
        for kk in range(TOP_K):
            pltpu.make_async_copy(src, xs_out.at[_row(slot_smem[0, t * TOP_K + kk])], sem).start()
        return carry

    lax.fori_loop(0, tm, issue, 0, unroll=4)
    for _ in range(TOP_K):
        pltpu.make_async_copy(x3_ref, xs_out.at[pl.ds(0, tm * ROW_SUB)], sem).wait()


def _dispatch(slots, x, xs):
    m, d = x.shape
    tm = MOE_TOKENS_PER_STEP
    return pl.pallas_call(
        _dispatch_kernel,
        grid=(m // tm,),
        in_specs=[pl.BlockSpec((1, 1, tm * TOP_K), lambda i: (i, 0, 0)),
                  pl.BlockSpec((tm, d), lambda i: (i, 0)),
                  pl.BlockSpec(memory_space=pl.ANY)],
        out_specs=pl.BlockSpec(memory_space=pl.ANY),
        out_shape=jax.ShapeDtypeStruct(xs.shape, xs.dtype),
        scratch_shapes=[pltpu.VMEM((tm * ROW_SUB, LANES), F32),
                        pltpu.SMEM((1, tm * TOP_K), jnp.int32),
                        pltpu.SemaphoreType.DMA,
                        pltpu.SemaphoreType.DMA],
        input_output_aliases={2: 0},
        compiler_params=_params("arbitrary"),
        name="moe_dispatch",
    )(slots, x, xs)


def _expert_kernel(te_ref, tv_ref, xs_ref, wgu_ref, bgu_ref, wd_ref, bd_ref, ys_ref):
    i = pl.program_id(0)

    @pl.when(tv_ref[i] == 0)
    def _():
        ys_ref[...] = jnp.zeros_like(ys_ref)

    @pl.when(tv_ref[i] != 0)
    def _():
        x = jnp.concatenate([xs_ref[pl.ds(j, MOE_TILE, stride=ROW_SUB), :] for j in range(ROW_SUB)],
                            axis=1).astype(BF16)
        gu = _dot(x, wgu_ref[...]) + bgu_ref[...]
        gate = jnp.minimum(gu[:, :D_EXPERT], SWIGLU_LIMIT)
        up = jnp.clip(gu[:, D_EXPERT:], -SWIGLU_LIMIT, SWIGLU_LIMIT)
        glu = gate * _sigmoid(gate * SWIGLU_ALPHA)
        y = _dot(((up + 1.0) * glu).astype(BF16), wd_ref[...]) + bd_ref[...]
        for j in range(ROW_SUB):
            ys_ref[pl.ds(j, MOE_TILE, stride=ROW_SUB), :] = y[:, j * LANES:(j + 1) * LANES]


def _experts(tile_e, tile_valid, xs, wgu, bgu, wd, bd):
    rows = xs.shape[0]
    ne, d = wgu.shape[0], wgu.shape[1]
    tile = pl.BlockSpec((MOE_TILE * ROW_SUB, LANES), lambda i, te, tv: (i, 0))

    def per_expert(shape):
        return pl.BlockSpec((None,) + shape, lambda i, te, tv: (te[i], 0, 0))

    return pl.pallas_call(
        _expert_kernel,
        grid_spec=pltpu.PrefetchScalarGridSpec(
            num_scalar_prefetch=2,
            grid=(rows // (MOE_TILE * ROW_SUB),),
            in_specs=[tile, per_expert((d, 2 * D_EXPERT)), per_expert((1, 2 * D_EXPERT)),
                      per_expert((D_EXPERT, d)), per_expert((1, d))],
            out_specs=tile),
        out_shape=jax.ShapeDtypeStruct(xs.shape, F32),
        compiler_params=_params("arbitrary"),
        name="moe_experts",
    )(tile_e, tile_valid, xs, wgu, bgu.reshape(ne, 1, -1).astype(F32), wd, bd.reshape(ne, 1, -1).astype(F32))


def _combine_kernel(slot_ref, ew_ref, x_ref, g_ref, b_ref, ys_ref, xo_ref, xbo_ref, buf_ref, slot_smem, sem, slot_sem):
    tm = x_ref.shape[0]
    _stage_slots(slot_ref, slot_smem, slot_sem)

    def issue(t, carry):
        for kk in range(TOP_K):
            pltpu.make_async_copy(ys_ref.at[_row(slot_smem[0, t * TOP_K + kk])], buf_ref.at[kk, _row(t)], sem).start()
        return carry

    lax.fori_loop(0, tm, issue, 0, unroll=4)
    for kk in range(TOP_K):
        pltpu.make_async_copy(ys_ref.at[pl.ds(0, tm * ROW_SUB)], buf_ref.at[kk], sem).wait()

    w = ew_ref[...]
    lane = lax.broadcasted_iota(jnp.int32, w.shape, 1)
    wk = [jnp.sum(jnp.where(lane == kk, w, 0.0), axis=-1, keepdims=True) for kk in range(TOP_K)]
    cols = []
    for j in range(ROW_SUB):
        rows = pl.ds(j, tm, stride=ROW_SUB)
        a = wk[0] * buf_ref[0, rows, :]
        for kk in range(1, TOP_K):
            a = a + wk[kk] * buf_ref[kk, rows, :]
        cols.append(a)
    x2 = _layer_norm(DEEPNORM_ALPHA * x_ref[...] + jnp.concatenate(cols, axis=1), g_ref[...], b_ref[...])
    xo_ref[...] = x2
    xbo_ref[...] = x2.astype(xbo_ref.dtype)


def _combine_ln(slots, ew, x, ys, ln_g, ln_b):
    m, d = x.shape
    tm = MOE_TOKENS_PER_STEP
    row = lambda i: (i, 0)
    fixed = lambda i: (0, 0)
    return pl.pallas_call(
        _combine_kernel,
        grid=(m // tm,),
        in_specs=[pl.BlockSpec((1, 1, tm * TOP_K), lambda i: (i, 0, 0)),
                  pl.BlockSpec((tm, LANES), row),
                  pl.BlockSpec((tm, d), row),
                  pl.BlockSpec((1, d), fixed),
                  pl.BlockSpec((1, d), fixed),
                  pl.BlockSpec(memory_space=pl.ANY)],
        out_specs=[pl.BlockSpec((tm, d), row), pl.BlockSpec((tm, d), row)],
        out_shape=[jax.ShapeDtypeStruct((m, d), F32), jax.ShapeDtypeStruct((m, d), BF16)],
        scratch_shapes=[pltpu.VMEM((TOP_K, tm * ROW_SUB, LANES), F32),
                        pltpu.SMEM((1, tm * TOP_K), jnp.int32),
                        pltpu.SemaphoreType.DMA,
                        pltpu.SemaphoreType.DMA],
        compiler_params=_params("arbitrary"),
        name="moe_combine_ln",
    )(slots, ew, x, ln_g.reshape(1, d).astype(F32), ln_b.reshape(1, d).astype(F32), ys)


def _moe_ln(x, eid, ew, xs, wgu, bgu, wd, bd, ln_g, ln_b):
    slots, tile_e, tile_valid = _moe_plan(eid, x.shape[0])
    xs = _dispatch(slots, x, xs)
    ys = _experts(tile_e, tile_valid, xs, wgu, bgu, wd, bd)
    x2, x2b = _combine_ln(slots, ew, x, ys, ln_g, ln_b)
    return x2, x2b, xs


def _rope_tables(seq):
    half = DA_HEAD_DIM // 2
    inv_freq = ROPE_THETA ** (-jnp.arange(half, dtype=F32) / half)
    ang = jnp.arange(seq, dtype=F32)[:, None] * inv_freq[None, :]
    cos = jnp.cos(ang)
    sin = jnp.sin(ang)
    return jnp.concatenate([cos, cos], axis=-1), jnp.concatenate([-sin, sin], axis=-1)


def _group_qkv_weights(w_qkvb, gi):
    part_w = DA_GROUPS * DA_WIDTH
    return jnp.concatenate([w_qkvb[:, p * part_w + gi * DA_WIDTH:p * part_w + (gi + 1) * DA_WIDTH]
                            for p in range(3)], axis=1)


def _layer(x, xb, xs, batch, seq, cosf, sinf, w_in, conv_w, a_log, dt_bias, dn_norm_w, w_branch_a, w_branch_b,
           w_out, ln1_g, ln1_b, router_w, router_b, w_gate_up, b_gate_up, w_down, b_down, ln2_g, ln2_b):
    w_qkvz = w_in[:, OFF_QKVZ:OFF_AB].astype(BF16)
    w_ab = w_in[:, OFF_AB:OFF_QKVB]
    w_qkvb = w_in[:, OFF_QKVB:OFF_GATES].astype(BF16)
    w_gates = w_in[:, OFF_GATES:].astype(BF16)

    qkvz = _matmul(xb, w_qkvz, BF16)
    gb = _dn_gates(x, w_ab, a_log, dt_bias)
    streams = [_matmul_streams(xb, _group_qkv_weights(w_qkvb, gi), cosf, sinf, batch, seq, dil)
               for gi, (_, dil) in enumerate(DA_PATTERNS)]
    gates = _matmul(xb, w_gates, BF16)

    y_a = _gdn(qkvz, gb, conv_w.astype(F32), dn_norm_w, batch, seq)
    y_b = _attention(streams, batch, seq)

    merged = _merge(y_a, y_b, w_branch_a.astype(BF16), w_branch_b.astype(BF16), gates)
    x1, eid, ew = _outproj_ln_router(merged, w_out.astype(BF16), x, ln1_g, ln1_b, router_w, router_b)
    return _moe_ln(x1, eid, ew, xs, w_gate_up.astype(BF16), b_gate_up, w_down.astype(BF16), b_down, ln2_g, ln2_b)


@jax.jit
def kernel(x, w_in, conv_w, a_log, dt_bias, dn_norm_w, w_branch_a, w_branch_b, w_out, ln1_g, ln1_b,
           router_w, router_b, w_gate_up, b_gate_up, w_down, b_down, ln2_g, ln2_b):
    batch, seq, d = x.shape
    cosf, sinf = _rope_tables(seq)
    xf = x.reshape(batch * seq, d)
    xb = xf.astype(BF16)
    xs = jnp.zeros((_moe_rows(batch * seq) * ROW_SUB, LANES), F32)
    for l in range(w_in.shape[0]):
        xf, xb, xs = _layer(xf, xb, xs, batch, seq, cosf, sinf, w_in[l], conv_w[l], a_log[l], dt_bias[l], dn_norm_w[l],
                        w_branch_a[l], w_branch_b[l], w_out[l], ln1_g[l], ln1_b[l], router_w[l], router_b[l],
                        w_gate_up[l], b_gate_up[l], w_down[l], b_down[l], ln2_g[l], ln2_b[l])
    return xf.reshape(batch, seq, d)
```

```python
import functools

import jax
import jax.numpy as jnp
from jax import lax
from jax.experimental import pallas as pl
from jax.experimental.pallas import tpu as pltpu

F32 = jnp.float32
BF16 = jnp.bfloat16
HIGHEST = lax.Precision.HIGHEST

D_MODEL = 2048
DEPTH = 4
DN_HEADS = 16
DN_HEAD_DIM = 128
DN_WIDTH = DN_HEADS * DN_HEAD_DIM
DN_CONV = 4
DN_CHUNK = 64
DA_PATTERNS = ((128, 1), (512, 4), (2048, 16))
DA_GROUPS = len(DA_PATTERNS)
DA_HEADS_PER_GROUP = 8
DA_HEAD_DIM = 128
DA_WIDTH = DA_HEADS_PER_GROUP * DA_HEAD_DIM
ROPE_THETA = 10000.0
N_EXPERTS = 32
TOP_K = 4
D_EXPERT = 512
SWIGLU_LIMIT = 7.0
SWIGLU_ALPHA = 1.702
DEEPNORM_ALPHA = (2 * DEPTH) ** 0.25
LN_EPS = 1e-5
NORM_EPS = 1e-6

LANES = 128
VMEM_LIMIT = 56 * 1024 * 1024

OFF_QKVZ = 0
OFF_AB = 4 * DN_WIDTH
OFF_QKVB = OFF_AB + 2 * DN_HEADS
OFF_GATES = OFF_QKVB + 3 * DA_GROUPS * DA_WIDTH
IN_WIDTH = OFF_GATES + 2 * D_MODEL


def _params(*sem):
    return pltpu.CompilerParams(dimension_semantics=sem, vmem_limit_bytes=VMEM_LIMIT)


def _sigmoid(v):
    return 1.0 / (1.0 + jnp.exp(-v))


def _dot(a, b):
    return jnp.dot(a, b, preferred_element_type=F32)


def _dot_f32(a, b):
    return jnp.dot(a, b, preferred_element_type=F32, precision=HIGHEST)


def _dot_nt(a, b):
    return lax.dot_general(a, b, (((1,), (1,)), ((), ())), preferred_element_type=F32)


def _dot_tn(a, b):
    return lax.dot_general(a, b, (((0,), (0,)), ((), ())), preferred_element_type=F32)


def _split_bf16(a):
    hi = a.astype(BF16)
    return hi, (a - hi.astype(F32)).astype(BF16)


def _dot_split(a, b):
    ah, al = a
    bh, bl = b
    return _dot(ah, bh) + (_dot(ah, bl) + _dot(al, bh))


def _mm_kernel(x_ref, w_ref, o_ref):
    o_ref[...] = _dot(x_ref[...], w_ref[...]).astype(o_ref.dtype)


def _matmul(x, w, out_dtype, tm=1024, tn=1024):
    m, k = x.shape
    n = w.shape[1]
    return pl.pallas_call(
        _mm_kernel,
        grid=(m // tm, n // tn),
        in_specs=[pl.BlockSpec((tm, k), lambda i, j: (i, 0)),
                  pl.BlockSpec((k, tn), lambda i, j: (0, j))],
        out_specs=pl.BlockSpec((tm, tn), lambda i, j: (i, j)),
        out_shape=jax.ShapeDtypeStruct((m, n), out_dtype),
        compiler_params=_params("parallel", "arbitrary"),
        name="matmul",
    )(x, w)


def _mm_stream_kernel(x_ref, w_ref, cos_ref, sin_ref, o_ref, acc_ref, rot_ref, *, dil, q_scale):
    j = pl.program_id(1)
    tm = x_ref.shape[0]
    rows_per = tm // dil
    dh = DA_HEAD_DIM
    half = dh // 2
    scale = jnp.where(j == 0, q_scale, 1.0).astype(F32)
    is_v = j == 2
    rot_ref[0] = jnp.where(is_v, 1.0, cos_ref[...] * scale)
    rot_ref[1] = jnp.where(is_v, 0.0, sin_ref[...] * scale)

    def rows(r):
        return pl.ds(r, rows_per, stride=dil) if dil > 1 else slice(None)

    x = x_ref[...]
    pair = 2 * dh
    for hp in range(DA_WIDTH // pair):
        acc = _dot(x, w_ref[:, hp * pair:(hp + 1) * pair])
        for h2 in range(2):
            acc_ref[2 * hp + h2] = acc[:, h2 * dh:(h2 + 1) * dh]
        for r in range(dil):
            cosf = rot_ref[0, rows(r), :]
            sinf = rot_ref[1, rows(r), :]
            for h2 in range(2):
                hh = 2 * hp + h2
                t = acc_ref[hh, rows(r), :]
                o_ref[r, :, hh * dh:(hh + 1) * dh] = (t * cosf + pltpu.roll(t, half, 1) * sinf).astype(o_ref.dtype)


def _matmul_streams(x, w, cosf, sinf, batch, seq, dil, tm=1024):
    m, k = x.shape
    nseq = seq // tm
    n = seq // dil
    kern = functools.partial(_mm_stream_kernel, dil=dil, q_scale=DA_HEAD_DIM ** -0.5)
    return pl.pallas_call(
        kern,
        grid=(m // tm, 3),
        in_specs=[pl.BlockSpec((tm, k), lambda i, j: (i, 0)),
                  pl.BlockSpec((k, DA_WIDTH), lambda i, j: (0, j)),
                  pl.BlockSpec((tm, DA_HEAD_DIM), lambda i, j: (i % nseq, 0)),
                  pl.BlockSpec((tm, DA_HEAD_DIM), lambda i, j: (i % nseq, 0))],
        out_specs=pl.BlockSpec((None, dil, tm // dil, DA_WIDTH), lambda i, j: (i // nseq, 0, i % nseq, j)),
        out_shape=jax.ShapeDtypeStruct((batch, dil, n, 3 * DA_WIDTH), BF16),
        scratch_shapes=[pltpu.VMEM((DA_HEADS_PER_GROUP, tm, DA_HEAD_DIM), F32),
                        pltpu.VMEM((2, tm, DA_HEAD_DIM), F32)],
        compiler_params=_params("parallel", "arbitrary"),
        name=f"qkv_streams_d{dil}",
    )(x, w, cosf, sinf)


def _gates_kernel(x_ref, w_ref, a_ref, dt_ref, o_ref):
    lin = _dot_f32(x_ref[...], w_ref[...])
    xa = lin + dt_ref[...]
    softplus = jnp.maximum(xa, 0.0) + jnp.log(1.0 + jnp.exp(-jnp.abs(xa)))
    g = -jnp.exp(a_ref[...]) * softplus
    beta = _sigmoid(lin)
    c = DN_CHUNK
    ri = lax.broadcasted_iota(jnp.int32, (c, c), 0)
    ci = lax.broadcasted_iota(jnp.int32, (c, c), 1)
    tri = (ri >= ci).astype(F32)
    gcum = jnp.concatenate([_dot_f32(tri, g[i * c:(i + 1) * c]) for i in range(lin.shape[0] // c)], axis=0)
    lane = lax.broadcasted_iota(jnp.int32, lin.shape, 1)
    o_ref[...] = jnp.where(lane < DN_HEADS, gcum, beta)


def _dn_gates(x, w_ab, a_log, dt_bias, tm=512):
    m, k = x.shape
    pad = LANES - 2 * DN_HEADS
    w = jnp.pad(w_ab, ((0, 0), (0, pad)))
    a = jnp.pad(a_log.astype(F32), (0, LANES - DN_HEADS)).reshape(1, LANES)
    dt = jnp.pad(dt_bias.astype(F32), (0, LANES - DN_HEADS)).reshape(1, LANES)
    return pl.pallas_call(
        _gates_kernel,
        grid=(m // tm,),
        in_specs=[pl.BlockSpec((tm, k), lambda i: (i, 0)),
                  pl.BlockSpec((k, LANES), lambda i: (0, 0)),
                  pl.BlockSpec((1, LANES), lambda i: (0, 0)),
                  pl.BlockSpec((1, LANES), lambda i: (0, 0))],
        out_specs=pl.BlockSpec((tm, LANES), lambda i: (i, 0)),
        out_shape=jax.ShapeDtypeStruct((m, LANES), F32),
        compiler_params=_params("parallel"),
        name="dn_gates",
    )(x, w, a, dt)


GDN_BLOCK = 512
GDN_HEADS_PER_STEP = 8
CARRY = 8


def _bdot(a, b):
    return lax.dot_general(a, b, (((2,), (1,)), ((0,), (0,))), preferred_element_type=F32)


def _bdot_nt(a, b):
    return lax.dot_general(a, b, (((2,), (2,)), ((0,), (0,))), preferred_element_type=F32)


def _unit_lower_inverse(lmat):
    c = lmat.shape[-1]
    ri = lax.broadcasted_iota(jnp.int32, lmat.shape, 1)
    ci = lax.broadcasted_iota(jnp.int32, lmat.shape, 2)
    eye = (ri == ci).astype(F32)
    s = 1
    minv = eye
    while s < c:
        shift = s.bit_length()
        off = ((ri >> shift) == (ci >> shift)) & ((ri & s) != 0) & ((ci & s) == 0)
        boff = jnp.where(off, lmat, 0.0)
        if s == 1:
            minv = eye - boff
        else:
            mb = minv.astype(BF16)
            minv = minv - _bdot(_bdot(mb, boff.astype(BF16)).astype(BF16), mb)
        s *= 2
    return minv


def _gdn_kernel(q_ref, k_ref, v_ref, z_ref, gb_ref, cwq_ref, cwk_ref, cwv_ref, nw_ref, o_ref,
                ext_ref, qkv_ref, gbc_ref, state_ref, u_ref, wq_ref, intra_ref, kd_ref, egl_ref, *, heads):
    hg = pl.program_id(1)
    t = pl.program_id(2)
    tb = q_ref.shape[0]
    dh = DN_HEAD_DIM
    c = DN_CHUNK
    nchunks = tb // c

    @pl.when(t == 0)
    def _():
        state_ref[...] = jnp.zeros_like(state_ref)
        ext_ref[:, 0:CARRY, :] = jnp.zeros((3, CARRY, ext_ref.shape[2]), F32)

    for idx, (ref, cw_ref) in enumerate(((q_ref, cwq_ref), (k_ref, cwk_ref), (v_ref, cwv_ref))):
        ext_ref[idx, CARRY:CARRY + tb, :] = ref[...].astype(F32)
        acc = ext_ref[idx, CARRY:CARRY + tb, :] * cw_ref[DN_CONV - 1:DN_CONV, :]
        for back in range(1, DN_CONV):
            acc = acc + (ext_ref[idx, CARRY - back:CARRY - back + tb, :]
                         * cw_ref[DN_CONV - 1 - back:DN_CONV - back, :])
        ext_ref[idx, 0:CARRY, :] = ext_ref[idx, tb:tb + CARRY, :]
        act = acc * _sigmoid(acc)
        for g in range(heads):
            a = act[:, g * dh:(g + 1) * dh]
            if idx < 2:
                a = a * lax.rsqrt(jnp.sum(a * a, axis=-1, keepdims=True) + NORM_EPS)
            if idx == 0:
                a = a * (dh ** -0.5)
            qkv_ref[idx, g] = a.reshape(nchunks, c, dh)

    gb = gb_ref[...]
    lane = lax.broadcasted_iota(jnp.int32, gb.shape, 1)
    for g in range(heads):
        h = hg * heads + g
        gcol = jnp.sum(jnp.where(lane == h, gb, 0.0), axis=-1, keepdims=True)
        bcol = jnp.sum(jnp.where(lane == h + DN_HEADS, gb, 0.0), axis=-1, keepdims=True)
        gbc_ref[0, g] = jnp.broadcast_to(gcol, (tb, dh)).reshape(nchunks, c, dh)
        gbc_ref[1, g] = jnp.broadcast_to(bcol, (tb, dh)).reshape(nchunks, c, dh)

    nb = heads * nchunks
    ri = lax.broadcasted_iota(jnp.int32, (nb, c, c), 1)
    ci = lax.broadcasted_iota(jnp.int32, (nb, c, c), 2)

    q = qkv_ref[0].reshape(nb, c, dh)
    k = qkv_ref[1].reshape(nb, c, dh)
    v = qkv_ref[2].reshape(nb, c, dh)
    gc = gbc_ref[0].reshape(nb, c, dh)
    beta = gbc_ref[1].reshape(nb, c, dh)
    gsq = gc[:, :, :c]
    diff = gsq - jnp.swapaxes(gsq, 1, 2)
    decay = jnp.where(ri >= ci, jnp.exp(jnp.minimum(diff, 0.0)), 0.0)
    kb = k * beta
    both = _bdot_nt(jnp.concatenate([q, kb], axis=1).astype(BF16), k.astype(BF16))
    lower = jnp.where(ri > ci, both[:, c:] * decay, 0.0)
    tinv = _unit_lower_inverse(lower)
    eg = jnp.exp(gc)
    rhs = jnp.concatenate([v * beta, kb * eg], axis=2).astype(BF16)
    uw = _bdot(tinv.astype(BF16), rhs)
    u_ref[...] = uw[:, :, :dh].reshape(heads, nchunks, c, dh)
    wq_ref[...] = jnp.concatenate([uw[:, :, dh:], q * eg], axis=1).astype(BF16).reshape(heads, nchunks, 2 * c, dh)
    intra_ref[...] = (both[:, :c] * decay).astype(BF16).reshape(heads, nchunks, c, c)
    g_last = gc[:, c - 1:c, :]
    kd_ref[...] = jnp.swapaxes(k * jnp.exp(g_last - gc), 1, 2).astype(BF16).reshape(heads, nchunks, dh, c)
    egl_ref[...] = jnp.broadcast_to(jnp.exp(g_last), (nb, CARRY, dh)).reshape(heads, nchunks, CARRY, dh)

    def chunk_body(ic, carry):
        state = state_ref[...]
        ws_qs = _bdot(wq_ref[:, ic], state.astype(BF16))
        vnb = (u_ref[:, ic] - ws_qs[:, :c]).astype(BF16)
        out = ws_qs[:, c:] + _bdot(intra_ref[:, ic], vnb)
        state_ref[...] = state * egl_ref[:, ic, 0:1, :] + _bdot(kd_ref[:, ic], vnb)
        u_ref[:, ic] = out
        return carry

    lax.fori_loop(0, nchunks, chunk_body, 0)

    nw = nw_ref[...]
    for g in range(heads):
        o = u_ref[g].reshape(tb, dh)
        o = o * lax.rsqrt(jnp.mean(o * o, axis=-1, keepdims=True) + NORM_EPS) * nw
        zz = z_ref[:, g * dh:(g + 1) * dh].astype(F32)
        o_ref[:, g * dh:(g + 1) * dh] = (o * (zz * _sigmoid(zz))).astype(o_ref.dtype)


def _gdn(qkvz, gb, conv_w, dn_norm_w, batch, seq):
    tb = GDN_BLOCK
    heads = GDN_HEADS_PER_STEP
    wblk = heads * DN_HEAD_DIM
    nhg = DN_HEADS // heads
    nblk = seq // tb
    nchunks = tb // DN_CHUNK
    tokens = batch * seq
    dh = DN_HEAD_DIM

    def col(part):
        return pl.BlockSpec((tb, wblk), lambda b, h, t: (b * nblk + t, part * nhg + h))

    def cw(part):
        return pl.BlockSpec((DN_CONV, wblk), lambda b, h, t: (0, part * nhg + h))

    kern = functools.partial(_gdn_kernel, heads=heads)
    return pl.pallas_call(
        kern,
        grid=(batch, nhg, nblk),
        in_specs=[col(0), col(1), col(2), col(3),
                  pl.BlockSpec((tb, LANES), lambda b, h, t: (b * nblk + t, 0)),
                  cw(0), cw(1), cw(2),
                  pl.BlockSpec((1, dh), lambda b, h, t: (0, 0))],
        out_specs=pl.BlockSpec((tb, wblk), lambda b, h, t: (b * nblk + t, h)),
        out_shape=jax.ShapeDtypeStruct((tokens, DN_WIDTH), BF16),
        scratch_shapes=[pltpu.VMEM((3, tb + CARRY, wblk), F32),
                        pltpu.VMEM((3, heads, nchunks, DN_CHUNK, dh), F32),
                        pltpu.VMEM((2, heads, nchunks, DN_CHUNK, dh), F32),
                        pltpu.VMEM((heads, dh, dh), F32),
                        pltpu.VMEM((heads, nchunks, DN_CHUNK, dh), F32),
                        pltpu.VMEM((heads, nchunks, 2 * DN_CHUNK, dh), BF16),
                        pltpu.VMEM((heads, nchunks, DN_CHUNK, DN_CHUNK), BF16),
                        pltpu.VMEM((heads, nchunks, dh, DN_CHUNK), BF16),
                        pltpu.VMEM((heads, nchunks, CARRY, dh), F32)],
        compiler_params=_params("parallel", "parallel", "arbitrary"),
        name="gdn",
    )(qkvz, qkvz, qkvz, qkvz, gb, conv_w, conv_w, conv_w, dn_norm_w.reshape(1, dh).astype(F32))


ATT_SPAN = 2048
ATT_HEADS_PER_STEP = 2
ATT_WIN = 128
ATT_UNITS = 4


def _attn_kernel(*refs):
    qkv = refs[:3 * DA_GROUPS]
    y_ref = refs[3 * DA_GROUPS]
    ext = refs[3 * DA_GROUPS + 1:3 * DA_GROUPS + 1 + 2 * DA_GROUPS]
    acc_ref, m_ref, l_ref = refs[3 * DA_GROUPS + 1 + 2 * DA_GROUPS:]
    span = pl.program_id(2)
    win = ATT_WIN
    dh = DA_HEAD_DIM
    hp = ATT_HEADS_PER_STEP

    row = lax.broadcasted_iota(jnp.int32, (win, 2 * win), 0)
    colm = lax.broadcasted_iota(jnp.int32, (win, 2 * win), 1)
    band = (colm >= row) & (colm <= row + win)

    for gi, (window, dil) in enumerate(DA_PATTERNS):
        q_ref, k_ref, v_ref = qkv[3 * gi:3 * gi + 3]
        kext, vext = ext[2 * gi:2 * gi + 2]
        n_rows = ATT_SPAN // dil
        nblk = n_rows // win

        @pl.when(span == 0)
        def _():
            kext[:, 0:win, :] = jnp.zeros((dil, win, hp * dh), kext.dtype)
            vext[:, 0:win, :] = jnp.zeros((dil, win, hp * dh), vext.dtype)

        kext[:, win:win + n_rows, :] = k_ref[...]
        vext[:, win:win + n_rows, :] = v_ref[...]

        def unit_pair(it, carry, gi=gi, dil=dil, nblk=nblk, q_ref=q_ref, kext=kext, vext=vext):
            work = []
            for sub in range(ATT_UNITS):
                u = ATT_UNITS * it + sub
                r = u // nblk
                jb = u % nblk
                r0 = pl.multiple_of(jb * win, win)
                mask = band & ((jb > 0) | (span > 0) | (colm >= win))
                if dil > 1:
                    nat = pl.ds(r + jb * (win * dil), win, stride=dil)
                else:
                    nat = pl.ds(r0, win)
                for hh in range(hp):
                    cs = slice(hh * dh, (hh + 1) * dh)
                    prev = None
                    if gi > 0:
                        prev = (acc_ref[hh, nat, :], m_ref[hh, nat, :], l_ref[hh, nat, :])
                    work.append((hh, nat, mask, q_ref[r, pl.ds(r0, win), cs], kext[r, pl.ds(r0, 2 * win), cs],
                                 vext[r, pl.ds(r0, 2 * win), cs], prev))
            qb = jnp.stack([w[3] for w in work])
            kw = jnp.stack([w[4] for w in work])
            vw = jnp.stack([w[5] for w in work])
            mask = jnp.stack([w[2] for w in work])
            sc = jnp.where(mask, _bdot_nt(qb, kw), -jnp.inf)
            mx = jnp.max(sc, axis=-1, keepdims=True)
            if gi == 0:
                m_new = jnp.broadcast_to(mx, qb.shape)
                p = jnp.exp(sc - mx)
                l_new = jnp.broadcast_to(jnp.sum(p, axis=-1, keepdims=True), qb.shape)
                acc = _bdot(p.astype(BF16), vw)
            else:
                acc_prev = jnp.stack([w[6][0] for w in work])
                m_prev = jnp.stack([w[6][1] for w in work])
                l_prev = jnp.stack([w[6][2] for w in work])
                m_new = jnp.maximum(m_prev, mx)
                alpha = jnp.exp(m_prev - m_new)
                p = jnp.exp(sc - m_new[:, :, 0:1])
                l_new = l_prev * alpha + jnp.sum(p, axis=-1, keepdims=True)
                acc = acc_prev * alpha + _bdot(p.astype(BF16), vw)
            for i, (hh, nat) in enumerate((w[0], w[1]) for w in work):
                acc_ref[hh, nat, :] = acc[i]
                m_ref[hh, nat, :] = m_new[i]
                l_ref[hh, nat, :] = l_new[i]
            return carry

        lax.fori_loop(0, dil * nblk // ATT_UNITS, unit_pair, 0)

        kext[:, 0:win, :] = kext[:, n_rows:n_rows + win, :]
        vext[:, 0:win, :] = vext[:, n_rows:n_rows + win, :]

    for hh in range(hp):
        y_ref[:, hh * dh:(hh + 1) * dh] = (acc_ref[hh] / l_ref[hh]).astype(y_ref.dtype)


def _attention(streams, batch, seq):
    hp = ATT_HEADS_PER_STEP
    wblk = hp * DA_HEAD_DIM
    nhp = DA_HEADS_PER_GROUP // hp
    nspan = seq // ATT_SPAN
    in_specs, args, scratch = [], [], []
    for gi, (window, dil) in enumerate(DA_PATTERNS):
        n_rows = ATT_SPAN // dil
        for part in range(3):
            in_specs.append(pl.BlockSpec((None, dil, n_rows, wblk),
                                         lambda b, h, s, part=part: (b, 0, s, part * nhp + h)))
            args.append(streams[gi])
        scratch += [pltpu.VMEM((dil, n_rows + ATT_WIN, wblk), BF16)] * 2
    scratch += [pltpu.VMEM((hp, ATT_SPAN, DA_HEAD_DIM), F32)] * 3
    return pl.pallas_call(
        _attn_kernel,
        grid=(batch, nhp, nspan),
        in_specs=in_specs,
        out_specs=pl.BlockSpec((ATT_SPAN, wblk), lambda b, h, s: (b * nspan + s, h)),
        out_shape=jax.ShapeDtypeStruct((batch * seq, DA_WIDTH), BF16),
        scratch_shapes=scratch,
        compiler_params=_params("parallel", "parallel", "arbitrary"),
        name="dilated_attn",
    )(*args)


def _merge_kernel(ya_ref, yb_ref, wa_ref, wb_ref, ga_ref, gb_ref, o_ref):
    a = _dot(ya_ref[...], wa_ref[...])
    b = _dot(yb_ref[...], wb_ref[...])
    o_ref[...] = (_sigmoid(ga_ref[...].astype(F32)) * a + _sigmoid(gb_ref[...].astype(F32)) * b).astype(o_ref.dtype)


def _merge(ya, yb, wa, wb, gates, tm=1024, tn=1024):
    m = ya.shape[0]
    n = wa.shape[1]
    nt = n // tn
    return pl.pallas_call(
        _merge_kernel,
        grid=(m // tm, nt),
        in_specs=[pl.BlockSpec((tm, ya.shape[1]), lambda i, j: (i, 0)),
                  pl.BlockSpec((tm, yb.shape[1]), lambda i, j: (i, 0)),
                  pl.BlockSpec((wa.shape[0], tn), lambda i, j: (0, j)),
                  pl.BlockSpec((wb.shape[0], tn), lambda i, j: (0, j)),
                  pl.BlockSpec((tm, tn), lambda i, j: (i, j)),
                  pl.BlockSpec((tm, tn), lambda i, j: (i, nt + j))],
        out_specs=pl.BlockSpec((tm, tn), lambda i, j: (i, j)),
        out_shape=jax.ShapeDtypeStruct((m, n), BF16),
        compiler_params=_params("parallel", "arbitrary"),
        name="merge",
    )(ya, yb, wa, wb, gates, gates)


def _layer_norm(y, g, b):
    mu = jnp.mean(y, axis=-1, keepdims=True)
    yc = y - mu
    var = jnp.mean(yc * yc, axis=-1, keepdims=True)
    return yc * lax.rsqrt(var + LN_EPS) * g + b


LN_ROWS = 64


def _outproj_kernel(m_ref, w_ref, x_ref, g_ref, b_ref, rw_ref, rb_ref, xo_ref, eid_ref, ew_ref, acc_ref):
    acc_ref[...] = _dot(m_ref[...], w_ref[...])
    gam = g_ref[...]
    bet = b_ref[...]

    def ln_rows(i, carry):
        rows = pl.ds(pl.multiple_of(i * LN_ROWS, LN_ROWS), LN_ROWS)
        xo_ref[rows, :] = _layer_norm(DEEPNORM_ALPHA * x_ref[rows, :] + acc_ref[rows, :], gam, bet)
        return carry

    lax.fori_loop(0, x_ref.shape[0] // LN_ROWS, ln_rows, 0)
    logits = _dot_split(_split_bf16(xo_ref[...]), _split_bf16(rw_ref[...])) + rb_ref[...]
    lane = lax.broadcasted_iota(jnp.int32, logits.shape, 1)
    work = jnp.where(lane < N_EXPERTS, logits, -jnp.inf)
    top = jnp.max(work, axis=-1, keepdims=True)
    ids = jnp.zeros(logits.shape, jnp.int32)
    vals = jnp.full(logits.shape, -jnp.inf, F32)
    for kk in range(TOP_K):
        mx = jnp.max(work, axis=-1, keepdims=True)
        first_idx = jnp.min(jnp.where(work == mx, lane, LANES), axis=-1, keepdims=True)
        ids = jnp.where(lane == kk, first_idx, ids)
        vals = jnp.where(lane == kk, mx, vals)
        work = jnp.where(lane == first_idx, -jnp.inf, work)
    e = jnp.exp(vals - top)
    eid_ref[...] = ids
    ew_ref[...] = e / jnp.sum(e, axis=-1, keepdims=True)


def _outproj_ln_router(merged, w_out, x, ln_g, ln_b, router_w, router_b, tm=512):
    m, d = x.shape
    rw = jnp.pad(router_w.astype(F32), ((0, 0), (0, LANES - N_EXPERTS)))
    rb = jnp.pad(router_b.astype(F32), (0, LANES - N_EXPERTS)).reshape(1, LANES)
    row = lambda i: (i, 0)
    fixed = lambda i: (0, 0)
    return pl.pallas_call(
        _outproj_kernel,
        grid=(m // tm,),
        in_specs=[pl.BlockSpec((tm, d), row),
                  pl.BlockSpec((d, d), fixed),
                  pl.BlockSpec((tm, d), row),
                  pl.BlockSpec((1, d), fixed),
                  pl.BlockSpec((1, d), fixed),
                  pl.BlockSpec((d, LANES), fixed),
                  pl.BlockSpec((1, LANES), fixed)],
        out_specs=[pl.BlockSpec((tm, d), row), pl.BlockSpec((tm, LANES), row), pl.BlockSpec((tm, LANES), row)],
        out_shape=[jax.ShapeDtypeStruct((m, d), F32), jax.ShapeDtypeStruct((m, LANES), jnp.int32),
                   jax.ShapeDtypeStruct((m, LANES), F32)],
        scratch_shapes=[pltpu.VMEM((tm, d), F32)],
        compiler_params=_params("parallel"),
        name="outproj_ln_router",
    )(merged, w_out, x, ln_g.reshape(1, d).astype(F32), ln_b.reshape(1, d).astype(F32), rw, rb)


MOE_TILE = 512
MOE_TOKENS_PER_STEP = 256
ROW_SUB = D_MODEL // LANES


def _moe_rows(tokens):
    return tokens * TOP_K + N_EXPERTS * MOE_TILE


def _moe_plan(eid, tokens):
    flat = eid[:, :TOP_K].reshape(-1)
    onehot = (flat[:, None] == jnp.arange(N_EXPERTS, dtype=jnp.int32)[None, :]).astype(jnp.int32)
    csum = jnp.cumsum(onehot, axis=0)
    counts = csum[-1]
    pcount = ((counts + MOE_TILE - 1) // MOE_TILE) * MOE_TILE
    pend = jnp.cumsum(pcount)
    start = pend - pcount
    slot = jnp.sum(onehot * (start[None, :] + csum - 1), axis=1)
    ntiles = _moe_rows(tokens) // MOE_TILE
    tile_start = jnp.arange(ntiles, dtype=jnp.int32) * MOE_TILE
    tile_valid = (tile_start < pend[-1]).astype(jnp.int32)
    last_start = jnp.maximum(pend[-1] - MOE_TILE, 0)
    tile_e = jnp.sum((jnp.minimum(tile_start, last_start)[:, None] >= pend[None, :]).astype(jnp.int32), axis=1)
    nstep = tokens // MOE_TOKENS_PER_STEP
    return slot.astype(jnp.int32).reshape(nstep, 1, MOE_TOKENS_PER_STEP * TOP_K), tile_e, tile_valid


def _row(i):
    return pl.ds(pl.multiple_of(i * ROW_SUB, ROW_SUB), ROW_SUB)


def _stage_slots(slot_ref, slot_smem, sem):
    cp = pltpu.make_async_copy(slot_ref.at[0], slot_smem, sem)
    cp.start()
    cp.wait()


def _dispatch_kernel(slot_ref, x_ref, xs_in, xs_out, x3_ref, slot_smem, sem, slot_sem):
    del xs_in
    tm = x_ref.shape[0]
    _stage_slots(slot_ref, slot_smem, slot_sem)
    for j in range(ROW_SUB):
        x3_ref[pl.ds(j, tm, stride=ROW_SUB), :] = x_ref[:, j * LANES:(j + 1) * LANES]

    def issue(t, carry):
        src = x3_ref.at[_row(t)]
        for kk in range(TOP_K):
            pltpu.make_async_copy(src, xs_out.at[_row(slot_smem[0, t * TOP_K + kk])], sem).start()
        return carry

    lax.fori_loop(0, tm, issue, 0, unroll=4)
    for _ in range(TOP_K):
        pltpu.make_async_copy(x3_ref, xs_out.at[pl.ds(0, tm * ROW_SUB)], sem).wait()


def _dispatch(slots, x, xs):
    m, d = x.shape
    tm = MOE_TOKENS_PER_STEP
    return pl.pallas_call(
        _dispatch_kernel,
        grid=(m // tm,),
        in_specs=[pl.BlockSpec((1, 1, tm * TOP_K), lambda i: (i, 0, 0)),
                  pl.BlockSpec((tm, d), lambda i: (i, 0)),
                  pl.BlockSpec(memory_space=pl.ANY)],
        out_specs=pl.BlockSpec(memory_space=pl.ANY),
        out_shape=jax.ShapeDtypeStruct(xs.shape, xs.dtype),
        scratch_shapes=[pltpu.VMEM((tm * ROW_SUB, LANES), F32),
                        pltpu.SMEM((1, tm * TOP_K), jnp.int32),
                        pltpu.SemaphoreType.DMA,
                        pltpu.SemaphoreType.DMA],
        input_output_aliases={2: 0},
        compiler_params=_params("arbitrary"),
        name="moe_dispatch",
    )(slots, x, xs)


def _expert_kernel(te_ref, tv_ref, xs_ref, wgu_ref, bgu_ref, wd_ref, bd_ref, ys_ref):
    i = pl.program_id(0)

    @pl.when(tv_ref[i] == 0)
    def _():
        ys_ref[...] = jnp.zeros_like(ys_ref)

    @pl.when(tv_ref[i] != 0)
    def _():
        x = jnp.concatenate([xs_ref[pl.ds(j, MOE_TILE, stride=ROW_SUB), :] for j in range(ROW_SUB)],
                            axis=1).astype(BF16)
        gu = _dot(x, wgu_ref[...]) + bgu_ref[...]
        gate = jnp.minimum(gu[:, :D_EXPERT], SWIGLU_LIMIT)
        up = jnp.clip(gu[:, D_EXPERT:], -SWIGLU_LIMIT, SWIGLU_LIMIT)
        glu = gate * _sigmoid(gate * SWIGLU_ALPHA)
        y = _dot(((up + 1.0) * glu).astype(BF16), wd_ref[...]) + bd_ref[...]
        for j in range(ROW_SUB):
            ys_ref[pl.ds(j, MOE_TILE, stride=ROW_SUB), :] = y[:, j * LANES:(j + 1) * LANES]


def _experts(tile_e, tile_valid, xs, wgu, bgu, wd, bd):
    rows = xs.shape[0]
    ne, d = wgu.shape[0], wgu.shape[1]
    tile = pl.BlockSpec((MOE_TILE * ROW_SUB, LANES), lambda i, te, tv: (i, 0))

    def per_expert(shape):
        return pl.BlockSpec((None,) + shape, lambda i, te, tv: (te[i], 0, 0))

    return pl.pallas_call(
        _expert_kernel,
        grid_spec=pltpu.PrefetchScalarGridSpec(
            num_scalar_prefetch=2,
            grid=(rows // (MOE_TILE * ROW_SUB),),
            in_specs=[tile, per_expert((d, 2 * D_EXPERT)), per_expert((1, 2 * D_EXPERT)),
                      per_expert((D_EXPERT, d)), per_expert((1, d))],
            out_specs=tile),
        out_shape=jax.ShapeDtypeStruct(xs.shape, F32),
        compiler_params=_params("arbitrary"),
        name="moe_experts",
    )(tile_e, tile_valid, xs, wgu, bgu.reshape(ne, 1, -1).astype(F32), wd, bd.reshape(ne, 1, -1).astype(F32))


def _combine_kernel(slot_ref, ew_ref, x_ref, g_ref, b_ref, ys_ref, xo_ref, xbo_ref, buf_ref, slot_smem, sem, slot_sem):
    tm = x_ref.shape[0]
    _stage_slots(slot_ref, slot_smem, slot_sem)

    def issue(t, carry):
        for kk in range(TOP_K):
            pltpu.make_async_copy(ys_ref.at[_row(slot_smem[0, t * TOP_K + kk])], buf_ref.at[kk, _row(t)], sem).start()
        return carry

    lax.fori_loop(0, tm, issue, 0, unroll=4)
    for kk in range(TOP_K):
        pltpu.make_async_copy(ys_ref.at[pl.ds(0, tm * ROW_SUB)], buf_ref.at[kk], sem).wait()

    w = ew_ref[...]
    lane = lax.broadcasted_iota(jnp.int32, w.shape, 1)
    wk = [jnp.sum(jnp.where(lane == kk, w, 0.0), axis=-1, keepdims=True) for kk in range(TOP_K)]
    cols = []
    for j in range(ROW_SUB):
        rows = pl.ds(j, tm, stride=ROW_SUB)
        a = wk[0] * buf_ref[0, rows, :]
        for kk in range(1, TOP_K):
            a = a + wk[kk] * buf_ref[kk, rows, :]
        cols.append(a)
    x2 = _layer_norm(DEEPNORM_ALPHA * x_ref[...] + jnp.concatenate(cols, axis=1), g_ref[...], b_ref[...])
    xo_ref[...] = x2
    xbo_ref[...] = x2.astype(xbo_ref.dtype)


def _combine_ln(slots, ew, x, ys, ln_g, ln_b):
    m, d = x.shape
    tm = MOE_TOKENS_PER_STEP
    row = lambda i: (i, 0)
    fixed = lambda i: (0, 0)
    return pl.pallas_call(
        _combine_kernel,
        grid=(m // tm,),
        in_specs=[pl.BlockSpec((1, 1, tm * TOP_K), lambda i: (i, 0, 0)),
                  pl.BlockSpec((tm, LANES), row),
                  pl.BlockSpec((tm, d), row),
                  pl.BlockSpec((1, d), fixed),
                  pl.BlockSpec((1, d), fixed),
                  pl.BlockSpec(memory_space=pl.ANY)],
        out_specs=[pl.BlockSpec((tm, d), row), pl.BlockSpec((tm, d), row)],
        out_shape=[jax.ShapeDtypeStruct((m, d), F32), jax.ShapeDtypeStruct((m, d), BF16)],
        scratch_shapes=[pltpu.VMEM((TOP_K, tm * ROW_SUB, LANES), F32),
                        pltpu.SMEM((1, tm * TOP_K), jnp.int32),
                        pltpu.SemaphoreType.DMA,
                        pltpu.SemaphoreType.DMA],
        compiler_params=_params("arbitrary"),
        name="moe_combine_ln",
    )(slots, ew, x, ln_g.reshape(1, d).astype(F32), ln_b.reshape(1, d).astype(F32), ys)


def _moe_ln(x, eid, ew, xs, wgu, bgu, wd, bd, ln_g, ln_b):
    slots, tile_e, tile_valid = _moe_plan(eid, x.shape[0])
    xs = _dispatch(slots, x, xs)
    ys = _experts(tile_e, tile_valid, xs, wgu, bgu, wd, bd)
    x2, x2b = _combine_ln(slots, ew, x, ys, ln_g, ln_b)
    return x2, x2b, xs


def _rope_tables(seq):
    half = DA_HEAD_DIM // 2
    inv_freq = ROPE_THETA ** (-jnp.arange(half, dtype=F32) / half)
    ang = jnp.arange(seq, dtype=F32)[:, None] * inv_freq[None, :]
    cos = jnp.cos(ang)
    sin = jnp.sin(ang)
    return jnp.concatenate([cos, cos], axis=-1), jnp.concatenate([-sin, sin], axis=-1)


def _group_qkv_weights(w_qkvb, gi):
    part_w = DA_GROUPS * DA_WIDTH
    return jnp.concatenate([w_qkvb[:, p * part_w + gi * DA_WIDTH:p * part_w + (gi + 1) * DA_WIDTH]
                            for p in range(3)], axis=1)


def _layer(x, xb, xs, batch, seq, cosf, sinf, w_in, conv_w, a_log, dt_bias, dn_norm_w, w_branch_a, w_branch_b,
           w_out, ln1_g, ln1_b, router_w, router_b, w_gate_up, b_gate_up, w_down, b_down, ln2_g, ln2_b):
    w_qkvz = w_in[:, OFF_QKVZ:OFF_AB].astype(BF16)
    w_ab = w_in[:, OFF_AB:OFF_QKVB]
    w_qkvb = w_in[:, OFF_QKVB:OFF_GATES].astype(BF16)
    w_gates = w_in[:, OFF_GATES:].astype(BF16)

    qkvz = _matmul(xb, w_qkvz, BF16)
    gb = _dn_gates(x, w_ab, a_log, dt_bias)
    streams = [_matmul_streams(xb, _group_qkv_weights(w_qkvb, gi), cosf, sinf, batch, seq, dil)
               for gi, (_, dil) in enumerate(DA_PATTERNS)]
    gates = _matmul(xb, w_gates, BF16)

    y_a = _gdn(qkvz, gb, conv_w.astype(F32), dn_norm_w, batch, seq)
    y_b = _attention(streams, batch, seq)

    merged = _merge(y_a, y_b, w_branch_a.astype(BF16), w_branch_b.astype(BF16), gates)
    x1, eid, ew = _outproj_ln_router(merged, w_out.astype(BF16), x, ln1_g, ln1_b, router_w, router_b)
    return _moe_ln(x1, eid, ew, xs, w_gate_up.astype(BF16), b_gate_up, w_down.astype(BF16), b_down, ln2_g, ln2_b)


@jax.jit
def kernel(x, w_in, conv_w, a_log, dt_bias, dn_norm_w, w_branch_a, w_branch_b, w_out, ln1_g, ln1_b,
           router_w, router_b, w_gate_up, b_gate_up, w_down, b_down, ln2_g, ln2_b):
    batch, seq, d = x.shape
    cosf, sinf = _rope_tables(seq)
    xf = x.reshape(batch * seq, d)
    xb = xf.astype(BF16)
    xs = jnp.zeros((_moe_rows(batch * seq) * ROW_SUB, LANES), F32)
    for l in range(w_in.shape[0]):
        xf, xb, xs = _layer(xf, xb, xs, batch, seq, cosf, sinf, w_in[l], conv_w[l], a_log[l], dt_bias[l], dn_norm_w[l],
                        w_branch_a[l], w_branch_b[l], w_out[l], ln1_g[l], ln1_b[l], router_w[l], router_b[l],
                        w_gate_up[l], b_gate_up[l], w_down[l], b_down[l], ln2_g[l], ln2_b[l])
    return xf.reshape(batch, seq, d)
```

```python
import functools

import jax
import jax.numpy as jnp
from jax import lax
from jax.experimental import pallas as pl
from jax.experimental.pallas import tpu as pltpu

F32 = jnp.float32
BF16 = jnp.bfloat16
HIGHEST = lax.Precision.HIGHEST

D_MODEL = 2048
DEPTH = 4
DN_HEADS = 16
DN_HEAD_DIM = 128
DN_WIDTH = DN_HEADS * DN_HEAD_DIM
DN_CONV = 4
DN_CHUNK = 64
DA_PATTERNS = ((128, 1), (512, 4), (2048, 16))
DA_GROUPS = len(DA_PATTERNS)
DA_HEADS_PER_GROUP = 8
DA_HEAD_DIM = 128
DA_WIDTH = DA_HEADS_PER_GROUP * DA_HEAD_DIM
ROPE_THETA = 10000.0
N_EXPERTS = 32
TOP_K = 4
D_EXPERT = 512
SWIGLU_LIMIT = 7.0
SWIGLU_ALPHA = 1.702
DEEPNORM_ALPHA = (2 * DEPTH) ** 0.25
LN_EPS = 1e-5
NORM_EPS = 1e-6

LANES = 128
VMEM_LIMIT = 56 * 1024 * 1024

OFF_QKVZ = 0
OFF_AB = 4 * DN_WIDTH
OFF_QKVB = OFF_AB + 2 * DN_HEADS
OFF_GATES = OFF_QKVB + 3 * DA_GROUPS * DA_WIDTH
IN_WIDTH = OFF_GATES + 2 * D_MODEL


def _params(*sem):
    return pltpu.CompilerParams(dimension_semantics=sem, vmem_limit_bytes=VMEM_LIMIT)


def _sigmoid(v):
    return 1.0 / (1.0 + jnp.exp(-v))


def _dot(a, b):
    return jnp.dot(a, b, preferred_element_type=F32)


def _dot_f32(a, b):
    return jnp.dot(a, b, preferred_element_type=F32, precision=HIGHEST)


def _dot_nt(a, b):
    return lax.dot_general(a, b, (((1,), (1,)), ((), ())), preferred_element_type=F32)


def _dot_tn(a, b):
    return lax.dot_general(a, b, (((0,), (0,)), ((), ())), preferred_element_type=F32)


def _split_bf16(a):
    hi = a.astype(BF16)
    return hi, (a - hi.astype(F32)).astype(BF16)


def _dot_split(a, b):
    ah, al = a
    bh, bl = b
    return _dot(ah, bh) + (_dot(ah, bl) + _dot(al, bh))


def _mm_kernel(x_ref, w_ref, o_ref):
    o_ref[...] = _dot(x_ref[...], w_ref[...]).astype(o_ref.dtype)


def _matmul(x, w, out_dtype, tm=1024, tn=1024):
    m, k = x.shape
    n = w.shape[1]
    return pl.pallas_call(
        _mm_kernel,
        grid=(m // tm, n // tn),
        in_specs=[pl.BlockSpec((tm, k), lambda i, j: (i, 0)),
                  pl.BlockSpec((k, tn), lambda i, j: (0, j))],
        out_specs=pl.BlockSpec((tm, tn), lambda i, j: (i, j)),
        out_shape=jax.ShapeDtypeStruct((m, n), out_dtype),
        compiler_params=_params("parallel", "arbitrary"),
        name="matmul",
    )(x, w)


def _mm_stream_kernel(x_ref, w_ref, cos_ref, sin_ref, o_ref, acc_ref, rot_ref, *, dil, q_scale):
    j = pl.program_id(1)
    tm = x_ref.shape[0]
    rows_per = tm // dil
    dh = DA_HEAD_DIM
    half = dh // 2
    scale = jnp.where(j == 0, q_scale, 1.0).astype(F32)
    is_v = j == 2
    rot_ref[0] = jnp.where(is_v, 1.0, cos_ref[...] * scale)
    rot_ref[1] = jnp.where(is_v, 0.0, sin_ref[...] * scale)

    def rows(r):
        return pl.ds(r, rows_per, stride=dil) if dil > 1 else slice(None)

    x = x_ref[...]
    pair = 2 * dh
    for hp in range(DA_WIDTH // pair):
        acc = _dot(x, w_ref[:, hp * pair:(hp + 1) * pair])
        for h2 in range(2):
            acc_ref[2 * hp + h2] = acc[:, h2 * dh:(h2 + 1) * dh]
        for r in range(dil):
            cosf = rot_ref[0, rows(r), :]
            sinf = rot_ref[1, rows(r), :]
            for h2 in range(2):
                hh = 2 * hp + h2
                t = acc_ref[hh, rows(r), :]
                o_ref[r, :, hh * dh:(hh + 1) * dh] = (t * cosf + pltpu.roll(t, half, 1) * sinf).astype(o_ref.dtype)


def _matmul_streams(x, w, cosf, sinf, batch, seq, dil, tm=1024):
    m, k = x.shape
    nseq = seq // tm
    n = seq // dil
    kern = functools.partial(_mm_stream_kernel, dil=dil, q_scale=DA_HEAD_DIM ** -0.5)
    return pl.pallas_call(
        kern,
        grid=(m // tm, 3),
        in_specs=[pl.BlockSpec((tm, k), lambda i, j: (i, 0)),
                  pl.BlockSpec((k, DA_WIDTH), lambda i, j: (0, j)),
                  pl.BlockSpec((tm, DA_HEAD_DIM), lambda i, j: (i % nseq, 0)),
                  pl.BlockSpec((tm, DA_HEAD_DIM), lambda i, j: (i % nseq, 0))],
        out_specs=pl.BlockSpec((None, dil, tm // dil, DA_WIDTH), lambda i, j: (i // nseq, 0, i % nseq, j)),
        out_shape=jax.ShapeDtypeStruct((batch, dil, n, 3 * DA_WIDTH), BF16),
        scratch_shapes=[pltpu.VMEM((DA_HEADS_PER_GROUP, tm, DA_HEAD_DIM), F32),
                        pltpu.VMEM((2, tm, DA_HEAD_DIM), F32)],
        compiler_params=_params("parallel", "arbitrary"),
        name=f"qkv_streams_d{dil}",
    )(x, w, cosf, sinf)


def _gates_kernel(x_ref, w_ref, a_ref, dt_ref, o_ref):
    lin = _dot_split(_split_bf16(x_ref[...]), _split_bf16(w_ref[...]))
    xa = lin + dt_ref[...]
    softplus = jnp.maximum(xa, 0.0) + jnp.log(1.0 + jnp.exp(-jnp.abs(xa)))
    g = -jnp.exp(a_ref[...]) * softplus
    beta = _sigmoid(lin)
    c = DN_CHUNK
    ri = lax.broadcasted_iota(jnp.int32, (c, c), 0)
    ci = lax.broadcasted_iota(jnp.int32, (c, c), 1)
    tri = (ri >= ci).astype(F32)
    gcum = jnp.concatenate([_dot_f32(tri, g[i * c:(i + 1) * c]) for i in range(lin.shape[0] // c)], axis=0)
    lane = lax.broadcasted_iota(jnp.int32, lin.shape, 1)
    o_ref[...] = jnp.where(lane < DN_HEADS, gcum, beta)


def _dn_gates(x, w_ab, a_log, dt_bias, tm=512):
    m, k = x.shape
    pad = LANES - 2 * DN_HEADS
    w = jnp.pad(w_ab, ((0, 0), (0, pad)))
    a = jnp.pad(a_log.astype(F32), (0, LANES - DN_HEADS)).reshape(1, LANES)
    dt = jnp.pad(dt_bias.astype(F32), (0, LANES - DN_HEADS)).reshape(1, LANES)
    return pl.pallas_call(
        _gates_kernel,
        grid=(m // tm,),
        in_specs=[pl.BlockSpec((tm, k), lambda i: (i, 0)),
                  pl.BlockSpec((k, LANES), lambda i: (0, 0)),
                  pl.BlockSpec((1, LANES), lambda i: (0, 0)),
                  pl.BlockSpec((1, LANES), lambda i: (0, 0))],
        out_specs=pl.BlockSpec((tm, LANES), lambda i: (i, 0)),
        out_shape=jax.ShapeDtypeStruct((m, LANES), F32),
        compiler_params=_params("parallel"),
        name="dn_gates",
    )(x, w, a, dt)


GDN_BLOCK = 512
GDN_HEADS_PER_STEP = 8
CARRY = 8


def _bdot(a, b):
    return lax.dot_general(a, b, (((2,), (1,)), ((0,), (0,))), preferred_element_type=F32)


def _bdot_nt(a, b):
    return lax.dot_general(a, b, (((2,), (2,)), ((0,), (0,))), preferred_element_type=F32)


def _unit_lower_inverse(lmat):
    c = lmat.shape[-1]
    ri = lax.broadcasted_iota(jnp.int32, lmat.shape, 1)
    ci = lax.broadcasted_iota(jnp.int32, lmat.shape, 2)
    eye = (ri == ci).astype(F32)
    s = 1
    minv = eye
    while s < c:
        shift = s.bit_length()
        off = ((ri >> shift) == (ci >> shift)) & ((ri & s) != 0) & ((ci & s) == 0)
        boff = jnp.where(off, lmat, 0.0)
        if s == 1:
            minv = eye - boff
        else:
            mb = minv.astype(BF16)
            minv = minv - _bdot(_bdot(mb, boff.astype(BF16)).astype(BF16), mb)
        s *= 2
    return minv


def _gdn_kernel(q_ref, k_ref, v_ref, z_ref, gb_ref, cwq_ref, cwk_ref, cwv_ref, nw_ref, o_ref,
                ext_ref, qkv_ref, gbc_ref, state_ref, u_ref, wq_ref, intra_ref, kd_ref, egl_ref, *, heads):
    hg = pl.program_id(1)
    t = pl.program_id(2)
    tb = q_ref.shape[0]
    dh = DN_HEAD_DIM
    c = DN_CHUNK
    nchunks = tb // c

    @pl.when(t == 0)
    def _():
        state_ref[...] = jnp.zeros_like(state_ref)
        ext_ref[:, 0:CARRY, :] = jnp.zeros((3, CARRY, ext_ref.shape[2]), F32)

    for idx, (ref, cw_ref) in enumerate(((q_ref, cwq_ref), (k_ref, cwk_ref), (v_ref, cwv_ref))):
        ext_ref[idx, CARRY:CARRY + tb, :] = ref[...].astype(F32)
        acc = ext_ref[idx, CARRY:CARRY + tb, :] * cw_ref[DN_CONV - 1:DN_CONV, :]
        for back in range(1, DN_CONV):
            acc = acc + (ext_ref[idx, CARRY - back:CARRY - back + tb, :]
                         * cw_ref[DN_CONV - 1 - back:DN_CONV - back, :])
        ext_ref[idx, 0:CARRY, :] = ext_ref[idx, tb:tb + CARRY, :]
        act = acc * _sigmoid(acc)
        for g in range(heads):
            a = act[:, g * dh:(g + 1) * dh]
            if idx < 2:
                a = a * lax.rsqrt(jnp.sum(a * a, axis=-1, keepdims=True) + NORM_EPS)
            if idx == 0:
                a = a * (dh ** -0.5)
            qkv_ref[idx, g] = a.reshape(nchunks, c, dh)

    gb = gb_ref[...]
    lane = lax.broadcasted_iota(jnp.int32, gb.shape, 1)
    for g in range(heads):
        h = hg * heads + g
        gcol = jnp.sum(jnp.where(lane == h, gb, 0.0), axis=-1, keepdims=True)
        bcol = jnp.sum(jnp.where(lane == h + DN_HEADS, gb, 0.0), axis=-1, keepdims=True)
        gbc_ref[0, g] = jnp.broadcast_to(gcol, (tb, dh)).reshape(nchunks, c, dh)
        gbc_ref[1, g] = jnp.broadcast_to(bcol, (tb, dh)).reshape(nchunks, c, dh)

    nb = heads * nchunks
    ri = lax.broadcasted_iota(jnp.int32, (nb, c, c), 1)
    ci = lax.broadcasted_iota(jnp.int32, (nb, c, c), 2)

    q = qkv_ref[0].reshape(nb, c, dh)
    k = qkv_ref[1].reshape(nb, c, dh)
    v = qkv_ref[2].reshape(nb, c, dh)
    gc = gbc_ref[0].reshape(nb, c, dh)
    beta = gbc_ref[1].reshape(nb, c, dh)
    gsq = gc[:, :, :c]
    diff = gsq - jnp.swapaxes(gsq, 1, 2)
    decay = jnp.where(ri >= ci, jnp.exp(jnp.minimum(diff, 0.0)), 0.0)
    kb = k * beta
    both = _bdot_nt(jnp.concatenate([q, kb], axis=1).astype(BF16), k.astype(BF16))
    lower = jnp.where(ri > ci, both[:, c:] * decay, 0.0)
    tinv = _unit_lower_inverse(lower)
    eg = jnp.exp(gc)
    rhs = jnp.concatenate([v * beta, kb * eg], axis=2).astype(BF16)
    uw = _bdot(tinv.astype(BF16), rhs)
    u_ref[...] = uw[:, :, :dh].reshape(heads, nchunks, c, dh)
    wq_ref[...] = jnp.concatenate([uw[:, :, dh:], q * eg], axis=1).astype(BF16).reshape(heads, nchunks, 2 * c, dh)
    intra_ref[...] = (both[:, :c] * decay).astype(BF16).reshape(heads, nchunks, c, c)
    g_last = gc[:, c - 1:c, :]
    kd_ref[...] = jnp.swapaxes(k * jnp.exp(g_last - gc), 1, 2).astype(BF16).reshape(heads, nchunks, dh, c)
    egl_ref[...] = jnp.broadcast_to(jnp.exp(g_last), (nb, CARRY, dh)).reshape(heads, nchunks, CARRY, dh)

    def chunk_body(ic, carry):
        state = state_ref[...]
        ws_qs = _bdot(wq_ref[:, ic], state.astype(BF16))
        vnb = (u_ref[:, ic] - ws_qs[:, :c]).astype(BF16)
        out = ws_qs[:, c:] + _bdot(intra_ref[:, ic], vnb)
        state_ref[...] = state * egl_ref[:, ic, 0:1, :] + _bdot(kd_ref[:, ic], vnb)
        u_ref[:, ic] = out
        return carry

    lax.fori_loop(0, nchunks, chunk_body, 0)

    nw = nw_ref[...]
    for g in range(heads):
        o = u_ref[g].reshape(tb, dh)
        o = o * lax.rsqrt(jnp.mean(o * o, axis=-1, keepdims=True) + NORM_EPS) * nw
        zz = z_ref[:, g * dh:(g + 1) * dh].astype(F32)
        o_ref[:, g * dh:(g + 1) * dh] = (o * (zz * _sigmoid(zz))).astype(o_ref.dtype)


def _gdn(qkvz, gb, conv_w, dn_norm_w, batch, seq):
    tb = GDN_BLOCK
    heads = GDN_HEADS_PER_STEP
    wblk = heads * DN_HEAD_DIM
    nhg = DN_HEADS // heads
    nblk = seq // tb
    nchunks = tb // DN_CHUNK
    tokens = batch * seq
    dh = DN_HEAD_DIM

    def col(part):
        return pl.BlockSpec((tb, wblk), lambda b, h, t: (b * nblk + t, part * nhg + h))

    def cw(part):
        return pl.BlockSpec((DN_CONV, wblk), lambda b, h, t: (0, part * nhg + h))

    kern = functools.partial(_gdn_kernel, heads=heads)
    return pl.pallas_call(
        kern,
        grid=(batch, nhg, nblk),
        in_specs=[col(0), col(1), col(2), col(3),
                  pl.BlockSpec((tb, LANES), lambda b, h, t: (b * nblk + t, 0)),
                  cw(0), cw(1), cw(2),
                  pl.BlockSpec((1, dh), lambda b, h, t: (0, 0))],
        out_specs=pl.BlockSpec((tb, wblk), lambda b, h, t: (b * nblk + t, h)),
        out_shape=jax.ShapeDtypeStruct((tokens, DN_WIDTH), BF16),
        scratch_shapes=[pltpu.VMEM((3, tb + CARRY, wblk), F32),
                        pltpu.VMEM((3, heads, nchunks, DN_CHUNK, dh), F32),
                        pltpu.VMEM((2, heads, nchunks, DN_CHUNK, dh), F32),
                        pltpu.VMEM((heads, dh, dh), F32),
                        pltpu.VMEM((heads, nchunks, DN_CHUNK, dh), F32),
                        pltpu.VMEM((heads, nchunks, 2 * DN_CHUNK, dh), BF16),
                        pltpu.VMEM((heads, nchunks, DN_CHUNK, DN_CHUNK), BF16),
                        pltpu.VMEM((heads, nchunks, dh, DN_CHUNK), BF16),
                        pltpu.VMEM((heads, nchunks, CARRY, dh), F32)],
        compiler_params=_params("parallel", "parallel", "arbitrary"),
        name="gdn",
    )(qkvz, qkvz, qkvz, qkvz, gb, conv_w, conv_w, conv_w, dn_norm_w.reshape(1, dh).astype(F32))


ATT_SPAN = 2048
ATT_HEADS_PER_STEP = 2
ATT_WIN = 128
ATT_UNITS = 4


def _attn_kernel(*refs):
    qkv = refs[:3 * DA_GROUPS]
    y_ref = refs[3 * DA_GROUPS]
    ext = refs[3 * DA_GROUPS + 1:3 * DA_GROUPS + 1 + 2 * DA_GROUPS]
    acc_ref, m_ref, l_ref = refs[3 * DA_GROUPS + 1 + 2 * DA_GROUPS:]
    span = pl.program_id(2)
    win = ATT_WIN
    dh = DA_HEAD_DIM
    hp = ATT_HEADS_PER_STEP

    row = lax.broadcasted_iota(jnp.int32, (win, 2 * win), 0)
    colm = lax.broadcasted_iota(jnp.int32, (win, 2 * win), 1)
    band = (colm >= row) & (colm <= row + win)

    for gi, (window, dil) in enumerate(DA_PATTERNS):
        q_ref, k_ref, v_ref = qkv[3 * gi:3 * gi + 3]
        kext, vext = ext[2 * gi:2 * gi + 2]
        n_rows = ATT_SPAN // dil
        nblk = n_rows // win

        @pl.when(span == 0)
        def _():
            kext[:, 0:win, :] = jnp.zeros((dil, win, hp * dh), kext.dtype)
            vext[:, 0:win, :] = jnp.zeros((dil, win, hp * dh), vext.dtype)

        kext[:, win:win + n_rows, :] = k_ref[...]
        vext[:, win:win + n_rows, :] = v_ref[...]

        def unit_pair(it, carry, gi=gi, dil=dil, nblk=nblk, q_ref=q_ref, kext=kext, vext=vext):
            work = []
            for sub in range(ATT_UNITS):
                u = ATT_UNITS * it + sub
                r = u // nblk
                jb = u % nblk
                r0 = pl.multiple_of(jb * win, win)
                mask = band & ((jb > 0) | (span > 0) | (colm >= win))
                if dil > 1:
                    nat = pl.ds(r + jb * (win * dil), win, stride=dil)
                else:
                    nat = pl.ds(r0, win)
                for hh in range(hp):
                    cs = slice(hh * dh, (hh + 1) * dh)
                    prev = None
                    if gi > 0:
                        prev = (acc_ref[hh, nat, :], m_ref[hh, nat, :], l_ref[hh, nat, :])
                    work.append((hh, nat, mask, q_ref[r, pl.ds(r0, win), cs], kext[r, pl.ds(r0, 2 * win), cs],
                                 vext[r, pl.ds(r0, 2 * win), cs], prev))
            qb = jnp.stack([w[3] for w in work])
            kw = jnp.stack([w[4] for w in work])
            vw = jnp.stack([w[5] for w in work])
            mask = jnp.stack([w[2] for w in work])
            sc = jnp.where(mask, _bdot_nt(qb, kw), -jnp.inf)
            mx = jnp.max(sc, axis=-1, keepdims=True)
            if gi == 0:
                m_new = jnp.broadcast_to(mx, qb.shape)
                p = jnp.exp(sc - mx)
                l_new = jnp.broadcast_to(jnp.sum(p, axis=-1, keepdims=True), qb.shape)
                acc = _bdot(p.astype(BF16), vw)
            else:
                acc_prev = jnp.stack([w[6][0] for w in work])
                m_prev = jnp.stack([w[6][1] for w in work])
                l_prev = jnp.stack([w[6][2] for w in work])
                m_new = jnp.maximum(m_prev, mx)
                alpha = jnp.exp(m_prev - m_new)
                p = jnp.exp(sc - m_new[:, :, 0:1])
                l_new = l_prev * alpha + jnp.sum(p, axis=-1, keepdims=True)
                acc = acc_prev * alpha + _bdot(p.astype(BF16), vw)
            for i, (hh, nat) in enumerate((w[0], w[1]) for w in work):
                acc_ref[hh, nat, :] = acc[i]
                m_ref[hh, nat, :] = m_new[i]
                l_ref[hh, nat, :] = l_new[i]
            return carry

        lax.fori_loop(0, dil * nblk // ATT_UNITS, unit_pair, 0)

        kext[:, 0:win, :] = kext[:, n_rows:n_rows + win, :]
        vext[:, 0:win, :] = vext[:, n_rows:n_rows + win, :]

    for hh in range(hp):
        y_ref[:, hh * dh:(hh + 1) * dh] = (acc_ref[hh] / l_ref[hh]).astype(y_ref.dtype)


def _attention(streams, batch, seq):
    hp = ATT_HEADS_PER_STEP
    wblk = hp * DA_HEAD_DIM
    nhp = DA_HEADS_PER_GROUP // hp
    nspan = seq // ATT_SPAN
    in_specs, args, scratch = [], [], []
    for gi, (window, dil) in enumerate(DA_PATTERNS):
        n_rows = ATT_SPAN // dil
        for part in range(3):
            in_specs.append(pl.BlockSpec((None, dil, n_rows, wblk),
                                         lambda b, h, s, part=part: (b, 0, s, part * nhp + h)))
            args.append(streams[gi])
        scratch += [pltpu.VMEM((dil, n_rows + ATT_WIN, wblk), BF16)] * 2
    scratch += [pltpu.VMEM((hp, ATT_SPAN, DA_HEAD_DIM), F32)] * 3
    return pl.pallas_call(
        _attn_kernel,
        grid=(batch, nhp, nspan),
        in_specs=in_specs,
        out_specs=pl.BlockSpec((ATT_SPAN, wblk), lambda b, h, s: (b * nspan + s, h)),
        out_shape=jax.ShapeDtypeStruct((batch * seq, DA_WIDTH), BF16),
        scratch_shapes=scratch,
        compiler_params=_params("parallel", "parallel", "arbitrary"),
        name="dilated_attn",
    )(*args)


def _merge_kernel(ya_ref, yb_ref, wa_ref, wb_ref, ga_ref, gb_ref, o_ref):
    a = _dot(ya_ref[...], wa_ref[...])
    b = _dot(yb_ref[...], wb_ref[...])
    o_ref[...] = (_sigmoid(ga_ref[...].astype(F32)) * a + _sigmoid(gb_ref[...].astype(F32)) * b).astype(o_ref.dtype)


def _merge(ya, yb, wa, wb, gates, tm=1024, tn=1024):
    m = ya.shape[0]
    n = wa.shape[1]
    nt = n // tn
    return pl.pallas_call(
        _merge_kernel,
        grid=(m // tm, nt),
        in_specs=[pl.BlockSpec((tm, ya.shape[1]), lambda i, j: (i, 0)),
                  pl.BlockSpec((tm, yb.shape[1]), lambda i, j: (i, 0)),
                  pl.BlockSpec((wa.shape[0], tn), lambda i, j: (0, j)),
                  pl.BlockSpec((wb.shape[0], tn), lambda i, j: (0, j)),
                  pl.BlockSpec((tm, tn), lambda i, j: (i, j)),
                  pl.BlockSpec((tm, tn), lambda i, j: (i, nt + j))],
        out_specs=pl.BlockSpec((tm, tn), lambda i, j: (i, j)),
        out_shape=jax.ShapeDtypeStruct((m, n), BF16),
        compiler_params=_params("parallel", "arbitrary"),
        name="merge",
    )(ya, yb, wa, wb, gates, gates)


def _layer_norm(y, g, b):
    mu = jnp.mean(y, axis=-1, keepdims=True)
    yc = y - mu
    var = jnp.mean(yc * yc, axis=-1, keepdims=True)
    return yc * lax.rsqrt(var + LN_EPS) * g + b


LN_ROWS = 64


def _outproj_kernel(m_ref, w_ref, x_ref, g_ref, b_ref, rw_ref, rb_ref, xo_ref, eid_ref, ew_ref, acc_ref):
    acc_ref[...] = _dot(m_ref[...], w_ref[...])
    gam = g_ref[...]
    bet = b_ref[...]

    def ln_rows(i, carry):
        rows = pl.ds(pl.multiple_of(i * LN_ROWS, LN_ROWS), LN_ROWS)
        xo_ref[rows, :] = _layer_norm(DEEPNORM_ALPHA * x_ref[rows, :] + acc_ref[rows, :], gam, bet)
        return carry

    lax.fori_loop(0, x_ref.shape[0] // LN_ROWS, ln_rows, 0)
    logits = _dot_split(_split_bf16(xo_ref[...]), _split_bf16(rw_ref[...])) + rb_ref[...]
    lane = lax.broadcasted_iota(jnp.int32, logits.shape, 1)
    work = jnp.where(lane < N_EXPERTS, logits, -jnp.inf)
    top = jnp.max(work, axis=-1, keepdims=True)
    ids = jnp.zeros(logits.shape, jnp.int32)
    vals = jnp.full(logits.shape, -jnp.inf, F32)
    for kk in range(TOP_K):
        mx = jnp.max(work, axis=-1, keepdims=True)
        first_idx = jnp.min(jnp.where(work == mx, lane, LANES), axis=-1, keepdims=True)
        ids = jnp.where(lane == kk, first_idx, ids)
        vals = jnp.where(lane == kk, mx, vals)
        work = jnp.where(lane == first_idx, -jnp.inf, work)
    e = jnp.exp(vals - top)
    eid_ref[...] = ids
    ew_ref[...] = e / jnp.sum(e, axis=-1, keepdims=True)


def _outproj_ln_router(merged, w_out, x, ln_g, ln_b, router_w, router_b, tm=512):
    m, d = x.shape
    rw = jnp.pad(router_w.astype(F32), ((0, 0), (0, LANES - N_EXPERTS)))
    rb = jnp.pad(router_b.astype(F32), (0, LANES - N_EXPERTS)).reshape(1, LANES)
    row = lambda i: (i, 0)
    fixed = lambda i: (0, 0)
    return pl.pallas_call(
        _outproj_kernel,
        grid=(m // tm,),
        in_specs=[pl.BlockSpec((tm, d), row),
                  pl.BlockSpec((d, d), fixed),
                  pl.BlockSpec((tm, d), row),
                  pl.BlockSpec((1, d), fixed),
                  pl.BlockSpec((1, d), fixed),
                  pl.BlockSpec((d, LANES), fixed),
                  pl.BlockSpec((1, LANES), fixed)],
        out_specs=[pl.BlockSpec((tm, d), row), pl.BlockSpec((tm, LANES), row), pl.BlockSpec((tm, LANES), row)],
        out_shape=[jax.ShapeDtypeStruct((m, d), F32), jax.ShapeDtypeStruct((m, LANES), jnp.int32),
                   jax.ShapeDtypeStruct((m, LANES), F32)],
        scratch_shapes=[pltpu.VMEM((tm, d), F32)],
        compiler_params=_params("parallel"),
        name="outproj_ln_router",
    )(merged, w_out, x, ln_g.reshape(1, d).astype(F32), ln_b.reshape(1, d).astype(F32), rw, rb)


MOE_TILE = 512
MOE_TOKENS_PER_STEP = 256
ROW_SUB = D_MODEL // LANES


def _moe_rows(tokens):
    return tokens * TOP_K + N_EXPERTS * MOE_TILE


def _moe_plan(eid, tokens):
    flat = eid[:, :TOP_K].reshape(-1)
    onehot = (flat[:, None] == jnp.arange(N_EXPERTS, dtype=jnp.int32)[None, :]).astype(jnp.int32)
    csum = jnp.cumsum(onehot, axis=0)
    counts = csum[-1]
    pcount = ((counts + MOE_TILE - 1) // MOE_TILE) * MOE_TILE
    pend = jnp.cumsum(pcount)
    start = pend - pcount
    slot = jnp.sum(onehot * (start[None, :] + csum - 1), axis=1)
    ntiles = _moe_rows(tokens) // MOE_TILE
    tile_start = jnp.arange(ntiles, dtype=jnp.int32) * MOE_TILE
    tile_valid = (tile_start < pend[-1]).astype(jnp.int32)
    last_start = jnp.maximum(pend[-1] - MOE_TILE, 0)
    tile_e = jnp.sum((jnp.minimum(tile_start, last_start)[:, None] >= pend[None, :]).astype(jnp.int32), axis=1)
    nstep = tokens // MOE_TOKENS_PER_STEP
    return slot.astype(jnp.int32).reshape(nstep, 1, MOE_TOKENS_PER_STEP * TOP_K), tile_e, tile_valid


def _row(i):
    return pl.ds(pl.multiple_of(i * ROW_SUB, ROW_SUB), ROW_SUB)


def _stage_slots(slot_ref, slot_smem, sem):
    cp = pltpu.make_async_copy(slot_ref.at[0], slot_smem, sem)
    cp.start()
    cp.wait()


def _dispatch_kernel(slot_ref, x_ref, xs_in, xs_out, x3_ref, slot_smem, sems, slot_sem):
    del xs_in
    i = pl.program_id(0)
    cur = i % 2
    tm = x_ref.shape[0]
    _stage_slots(slot_ref, slot_smem, slot_sem)
    stage = x3_ref.at[cur]
    for j in range(ROW_SUB):
        stage[pl.ds(j, tm, stride=ROW_SUB), :] = x_ref[:, j * LANES:(j + 1) * LANES]

    def issue(t, carry):
        src = stage.at[_row(t)]
        for kk in range(TOP_K):
            pltpu.make_async_copy(src, xs_out.at[_row(slot_smem[0, t * TOP_K + kk])], sems.at[cur]).start()
        return carry

    lax.fori_loop(0, tm, issue, 0, unroll=4)

    def drain(buf):
        for _ in range(TOP_K):
            pltpu.make_async_copy(x3_ref.at[buf], xs_out.at[pl.ds(0, tm * ROW_SUB)], sems.at[buf]).wait()

    @pl.when(i > 0)
    def _():
        drain(1 - cur)

    @pl.when(i == pl.num_programs(0) - 1)
    def _():
        drain(cur)


def _dispatch(slots, x, xs):
    m, d = x.shape
    tm = MOE_TOKENS_PER_STEP
    return pl.pallas_call(
        _dispatch_kernel,
        grid=(m // tm,),
        in_specs=[pl.BlockSpec((1, 1, tm * TOP_K), lambda i: (i, 0, 0)),
                  pl.BlockSpec((tm, d), lambda i: (i, 0)),
                  pl.BlockSpec(memory_space=pl.ANY)],
        out_specs=pl.BlockSpec(memory_space=pl.ANY),
        out_shape=jax.ShapeDtypeStruct(xs.shape, xs.dtype),
        scratch_shapes=[pltpu.VMEM((2, tm * ROW_SUB, LANES), F32),
                        pltpu.SMEM((1, tm * TOP_K), jnp.int32),
                        pltpu.SemaphoreType.DMA((2,)),
                        pltpu.SemaphoreType.DMA],
        input_output_aliases={2: 0},
        compiler_params=_params("arbitrary"),
        name="moe_dispatch",
    )(slots, x, xs)


def _expert_kernel(te_ref, tv_ref, xs_ref, wgu_ref, bgu_ref, wd_ref, bd_ref, ys_ref):
    i = pl.program_id(0)

    @pl.when(tv_ref[i] == 0)
    def _():
        ys_ref[...] = jnp.zeros_like(ys_ref)

    @pl.when(tv_ref[i] != 0)
    def _():
        x = jnp.concatenate([xs_ref[pl.ds(j, MOE_TILE, stride=ROW_SUB), :] for j in range(ROW_SUB)],
                            axis=1).astype(BF16)
        gu = _dot(x, wgu_ref[...]) + bgu_ref[...]
        gate = jnp.minimum(gu[:, :D_EXPERT], SWIGLU_LIMIT)
        up = jnp.clip(gu[:, D_EXPERT:], -SWIGLU_LIMIT, SWIGLU_LIMIT)
        glu = gate * _sigmoid(gate * SWIGLU_ALPHA)
        y = _dot(((up + 1.0) * glu).astype(BF16), wd_ref[...]) + bd_ref[...]
        for j in range(ROW_SUB):
            ys_ref[pl.ds(j, MOE_TILE, stride=ROW_SUB), :] = y[:, j * LANES:(j + 1) * LANES]


def _experts(tile_e, tile_valid, xs, wgu, bgu, wd, bd):
    rows = xs.shape[0]
    ne, d = wgu.shape[0], wgu.shape[1]
    tile = pl.BlockSpec((MOE_TILE * ROW_SUB, LANES), lambda i, te, tv: (i, 0))

    def per_expert(shape):
        return pl.BlockSpec((None,) + shape, lambda i, te, tv: (te[i], 0, 0))

    return pl.pallas_call(
        _expert_kernel,
        grid_spec=pltpu.PrefetchScalarGridSpec(
            num_scalar_prefetch=2,
            grid=(rows // (MOE_TILE * ROW_SUB),),
            in_specs=[tile, per_expert((d, 2 * D_EXPERT)), per_expert((1, 2 * D_EXPERT)),
                      per_expert((D_EXPERT, d)), per_expert((1, d))],
            out_specs=tile),
        out_shape=jax.ShapeDtypeStruct(xs.shape, F32),
        compiler_params=_params("arbitrary"),
        name="moe_experts",
    )(tile_e, tile_valid, xs, wgu, bgu.reshape(ne, 1, -1).astype(F32), wd, bd.reshape(ne, 1, -1).astype(F32))


def _combine_kernel(slot_ref, next_slot_ref, ew_ref, x_ref, g_ref, b_ref, ys_ref, xo_ref, xbo_ref,
                    buf_ref, slot_smem, sems, slot_sem):
    i = pl.program_id(0)
    cur = i % 2
    tm = x_ref.shape[0]

    def gather(slots_ref, buf):
        _stage_slots(slots_ref, slot_smem, slot_sem)

        def issue(t, carry):
            for kk in range(TOP_K):
                pltpu.make_async_copy(ys_ref.at[_row(slot_smem[0, t * TOP_K + kk])],
                                      buf_ref.at[buf, kk, _row(t)], sems.at[buf]).start()
            return carry

        lax.fori_loop(0, tm, issue, 0, unroll=4)

    @pl.when(i == 0)
    def _():
        gather(slot_ref, cur)

    @pl.when(i + 1 < pl.num_programs(0))
    def _():
        gather(next_slot_ref, 1 - cur)

    for kk in range(TOP_K):
        pltpu.make_async_copy(ys_ref.at[pl.ds(0, tm * ROW_SUB)], buf_ref.at[cur, kk], sems.at[cur]).wait()

    w = ew_ref[...]
    lane = lax.broadcasted_iota(jnp.int32, w.shape, 1)
    wk = [jnp.sum(jnp.where(lane == kk, w, 0.0), axis=-1, keepdims=True) for kk in range(TOP_K)]
    cols = []
    for j in range(ROW_SUB):
        rows = pl.ds(j, tm, stride=ROW_SUB)
        a = wk[0] * buf_ref[cur, 0, rows, :]
        for kk in range(1, TOP_K):
            a = a + wk[kk] * buf_ref[cur, kk, rows, :]
        cols.append(a)
    x2 = _layer_norm(DEEPNORM_ALPHA * x_ref[...] + jnp.concatenate(cols, axis=1), g_ref[...], b_ref[...])
    xo_ref[...] = x2
    xbo_ref[...] = x2.astype(xbo_ref.dtype)


def _combine_ln(slots, ew, x, ys, ln_g, ln_b):
    m, d = x.shape
    tm = MOE_TOKENS_PER_STEP
    row = lambda i: (i, 0)
    fixed = lambda i: (0, 0)
    return pl.pallas_call(
        _combine_kernel,
        grid=(m // tm,),
        in_specs=[pl.BlockSpec((1, 1, tm * TOP_K), lambda i: (i, 0, 0)),
                  pl.BlockSpec((1, 1, tm * TOP_K), lambda i: (jnp.minimum(i + 1, m // tm - 1), 0, 0)),
                  pl.BlockSpec((tm, LANES), row),
                  pl.BlockSpec((tm, d), row),
                  pl.BlockSpec((1, d), fixed),
                  pl.BlockSpec((1, d), fixed),
                  pl.BlockSpec(memory_space=pl.ANY)],
        out_specs=[pl.BlockSpec((tm, d), row), pl.BlockSpec((tm, d), row)],
        out_shape=[jax.ShapeDtypeStruct((m, d), F32), jax.ShapeDtypeStruct((m, d), BF16)],
        scratch_shapes=[pltpu.VMEM((2, TOP_K, tm * ROW_SUB, LANES), F32),
                        pltpu.SMEM((1, tm * TOP_K), jnp.int32),
                        pltpu.SemaphoreType.DMA((2,)),
                        pltpu.SemaphoreType.DMA],
        compiler_params=_params("arbitrary"),
        name="moe_combine_ln",
    )(slots, slots, ew, x, ln_g.reshape(1, d).astype(F32), ln_b.reshape(1, d).astype(F32), ys)


def _moe_ln(x, eid, ew, xs, wgu, bgu, wd, bd, ln_g, ln_b):
    slots, tile_e, tile_valid = _moe_plan(eid, x.shape[0])
    xs = _dispatch(slots, x, xs)
    ys = _experts(tile_e, tile_valid, xs, wgu, bgu, wd, bd)
    x2, x2b = _combine_ln(slots, ew, x, ys, ln_g, ln_b)
    return x2, x2b, xs


def _rope_tables(seq):
    half = DA_HEAD_DIM // 2
    inv_freq = ROPE_THETA ** (-jnp.arange(half, dtype=F32) / half)
    ang = jnp.arange(seq, dtype=F32)[:, None] * inv_freq[None, :]
    cos = jnp.cos(ang)
    sin = jnp.sin(ang)
    return jnp.concatenate([cos, cos], axis=-1), jnp.concatenate([-sin, sin], axis=-1)


def _group_qkv_weights(w_qkvb, gi):
    part_w = DA_GROUPS * DA_WIDTH
    return jnp.concatenate([w_qkvb[:, p * part_w + gi * DA_WIDTH:p * part_w + (gi + 1) * DA_WIDTH]
                            for p in range(3)], axis=1)


def _layer(x, xb, xs, batch, seq, cosf, sinf, w_in, conv_w, a_log, dt_bias, dn_norm_w, w_branch_a, w_branch_b,
           w_out, ln1_g, ln1_b, router_w, router_b, w_gate_up, b_gate_up, w_down, b_down, ln2_g, ln2_b):
    w_qkvz = w_in[:, OFF_QKVZ:OFF_AB].astype(BF16)
    w_ab = w_in[:, OFF_AB:OFF_QKVB]
    w_qkvb = w_in[:, OFF_QKVB:OFF_GATES].astype(BF16)
    w_gates = w_in[:, OFF_GATES:].astype(BF16)

    qkvz = _matmul(xb, w_qkvz, BF16)
    gb = _dn_gates(x, w_ab, a_log, dt_bias)
    streams = [_matmul_streams(xb, _group_qkv_weights(w_qkvb, gi), cosf, sinf, batch, seq, dil)
               for gi, (_, dil) in enumerate(DA_PATTERNS)]
    gates = _matmul(xb, w_gates, BF16)

    y_a = _gdn(qkvz, gb, conv_w.astype(F32), dn_norm_w, batch, seq)
    y_b = _attention(streams, batch, seq)

    merged = _merge(y_a, y_b, w_branch_a.astype(BF16), w_branch_b.astype(BF16), gates)
    x1, eid, ew = _outproj_ln_router(merged, w_out.astype(BF16), x, ln1_g, ln1_b, router_w, router_b)
    return _moe_ln(x1, eid, ew, xs, w_gate_up.astype(BF16), b_gate_up, w_down.astype(BF16), b_down, ln2_g, ln2_b)


@jax.jit
def kernel(x, w_in, conv_w, a_log, dt_bias, dn_norm_w, w_branch_a, w_branch_b, w_out, ln1_g, ln1_b,
           router_w, router_b, w_gate_up, b_gate_up, w_down, b_down, ln2_g, ln2_b):
    batch, seq, d = x.shape
    cosf, sinf = _rope_tables(seq)
    xf = x.reshape(batch * seq, d)
    xb = xf.astype(BF16)
    xs = jnp.zeros((_moe_rows(batch * seq) * ROW_SUB, LANES), F32)
    for l in range(w_in.shape[0]):
        xf, xb, xs = _layer(xf, xb, xs, batch, seq, cosf, sinf, w_in[l], conv_w[l], a_log[l], dt_bias[l], dn_norm_w[l],
                        w_branch_a[l], w_branch_b[l], w_out[l], ln1_g[l], ln1_b[l], router_w[l], router_b[l],
                        w_gate_up[l], b_gate_up[l], w_down[l], b_down[l], ln2_g[l], ln2_b[l])
    return xf.reshape(batch, seq, d)
```

```python
import functools

import jax
import jax.numpy as jnp
from jax import lax
from jax.experimental import pallas as pl
from jax.experimental.pallas import tpu as pltpu

F32 = jnp.float32
BF16 = jnp.bfloat16
HIGHEST = lax.Precision.HIGHEST

D_MODEL = 2048
DEPTH = 4
DN_HEADS = 16
DN_HEAD_DIM = 128
DN_WIDTH = DN_HEADS * DN_HEAD_DIM
DN_CONV = 4
DN_CHUNK = 64
DA_PATTERNS = ((128, 1), (512, 4), (2048, 16))
DA_GROUPS = len(DA_PATTERNS)
DA_HEADS_PER_GROUP = 8
DA_HEAD_DIM = 128
DA_WIDTH = DA_HEADS_PER_GROUP * DA_HEAD_DIM
ROPE_THETA = 10000.0
N_EXPERTS = 32
TOP_K = 4
D_EXPERT = 512
SWIGLU_LIMIT = 7.0
SWIGLU_ALPHA = 1.702
DEEPNORM_ALPHA = (2 * DEPTH) ** 0.25
LN_EPS = 1e-5
NORM_EPS = 1e-6

LANES = 128
VMEM_LIMIT = 56 * 1024 * 1024

OFF_QKVZ = 0
OFF_AB = 4 * DN_WIDTH
OFF_QKVB = OFF_AB + 2 * DN_HEADS
OFF_GATES = OFF_QKVB + 3 * DA_GROUPS * DA_WIDTH
IN_WIDTH = OFF_GATES + 2 * D_MODEL


def _params(*sem):
    return pltpu.CompilerParams(dimension_semantics=sem, vmem_limit_bytes=VMEM_LIMIT)


def _sigmoid(v):
    return 1.0 / (1.0 + jnp.exp(-v))


def _dot(a, b):
    return jnp.dot(a, b, preferred_element_type=F32)


def _dot_f32(a, b):
    return jnp.dot(a, b, preferred_element_type=F32, precision=HIGHEST)


def _dot_nt(a, b):
    return lax.dot_general(a, b, (((1,), (1,)), ((), ())), preferred_element_type=F32)


def _dot_tn(a, b):
    return lax.dot_general(a, b, (((0,), (0,)), ((), ())), preferred_element_type=F32)


def _split_bf16(a):
    hi = a.astype(BF16)
    return hi, (a - hi.astype(F32)).astype(BF16)


def _dot_split(a, b):
    ah, al = a
    bh, bl = b
    return _dot(ah, bh) + (_dot(ah, bl) + _dot(al, bh))


def _mm_kernel(x_ref, w_ref, o_ref):
    o_ref[...] = _dot(x_ref[...], w_ref[...]).astype(o_ref.dtype)


def _matmul(x, w, out_dtype, tm=1024, tn=1024):
    m, k = x.shape
    n = w.shape[1]
    return pl.pallas_call(
        _mm_kernel,
        grid=(m // tm, n // tn),
        in_specs=[pl.BlockSpec((tm, k), lambda i, j: (i, 0)),
                  pl.BlockSpec((k, tn), lambda i, j: (0, j))],
        out_specs=pl.BlockSpec((tm, tn), lambda i, j: (i, j)),
        out_shape=jax.ShapeDtypeStruct((m, n), out_dtype),
        compiler_params=_params("parallel", "arbitrary"),
        name="matmul",
    )(x, w)


def _mm_stream_kernel(x_ref, w_ref, cos_ref, sin_ref, o_ref, acc_ref, rot_ref, *, dil, q_scale):
    j = pl.program_id(1)
    tm = x_ref.shape[0]
    rows_per = tm // dil
    dh = DA_HEAD_DIM
    half = dh // 2
    scale = jnp.where(j == 0, q_scale, 1.0).astype(F32)
    is_v = j == 2
    rot_ref[0] = jnp.where(is_v, 1.0, cos_ref[...] * scale)
    rot_ref[1] = jnp.where(is_v, 0.0, sin_ref[...] * scale)

    def rows(r):
        return pl.ds(r, rows_per, stride=dil) if dil > 1 else slice(None)

    x = x_ref[...]
    pair = 2 * dh
    for hp in range(DA_WIDTH // pair):
        acc = _dot(x, w_ref[:, hp * pair:(hp + 1) * pair])
        for h2 in range(2):
            acc_ref[2 * hp + h2] = acc[:, h2 * dh:(h2 + 1) * dh]
        for r in range(dil):
            cosf = rot_ref[0, rows(r), :]
            sinf = rot_ref[1, rows(r), :]
            for h2 in range(2):
                hh = 2 * hp + h2
                t = acc_ref[hh, rows(r), :]
                o_ref[r, :, hh * dh:(hh + 1) * dh] = (t * cosf + pltpu.roll(t, half, 1) * sinf).astype(o_ref.dtype)


def _matmul_streams(x, w, cosf, sinf, batch, seq, dil, tm=1024):
    m, k = x.shape
    nseq = seq // tm
    n = seq // dil
    kern = functools.partial(_mm_stream_kernel, dil=dil, q_scale=DA_HEAD_DIM ** -0.5)
    return pl.pallas_call(
        kern,
        grid=(m // tm, 3),
        in_specs=[pl.BlockSpec((tm, k), lambda i, j: (i, 0)),
                  pl.BlockSpec((k, DA_WIDTH), lambda i, j: (0, j)),
                  pl.BlockSpec((tm, DA_HEAD_DIM), lambda i, j: (i % nseq, 0)),
                  pl.BlockSpec((tm, DA_HEAD_DIM), lambda i, j: (i % nseq, 0))],
        out_specs=pl.BlockSpec((None, dil, tm // dil, DA_WIDTH), lambda i, j: (i // nseq, 0, i % nseq, j)),
        out_shape=jax.ShapeDtypeStruct((batch, dil, n, 3 * DA_WIDTH), BF16),
        scratch_shapes=[pltpu.VMEM((DA_HEADS_PER_GROUP, tm, DA_HEAD_DIM), F32),
                        pltpu.VMEM((2, tm, DA_HEAD_DIM), F32)],
        compiler_params=_params("parallel", "arbitrary"),
        name=f"qkv_streams_d{dil}",
    )(x, w, cosf, sinf)


def _gates_kernel(x_ref, w_ref, a_ref, dt_ref, o_ref):
    lin = _dot_split(_split_bf16(x_ref[...]), _split_bf16(w_ref[...]))
    xa = lin + dt_ref[...]
    softplus = jnp.maximum(xa, 0.0) + jnp.log(1.0 + jnp.exp(-jnp.abs(xa)))
    g = -jnp.exp(a_ref[...]) * softplus
    beta = _sigmoid(lin)
    c = DN_CHUNK
    ri = lax.broadcasted_iota(jnp.int32, (c, c), 0)
    ci = lax.broadcasted_iota(jnp.int32, (c, c), 1)
    tri = (ri >= ci).astype(F32)
    gcum = jnp.concatenate([_dot_f32(tri, g[i * c:(i + 1) * c]) for i in range(lin.shape[0] // c)], axis=0)
    lane = lax.broadcasted_iota(jnp.int32, lin.shape, 1)
    o_ref[...] = jnp.where(lane < DN_HEADS, gcum, beta)


def _dn_gates(x, w_ab, a_log, dt_bias, tm=512):
    m, k = x.shape
    pad = LANES - 2 * DN_HEADS
    w = jnp.pad(w_ab, ((0, 0), (0, pad)))
    a = jnp.pad(a_log.astype(F32), (0, LANES - DN_HEADS)).reshape(1, LANES)
    dt = jnp.pad(dt_bias.astype(F32), (0, LANES - DN_HEADS)).reshape(1, LANES)
    return pl.pallas_call(
        _gates_kernel,
        grid=(m // tm,),
        in_specs=[pl.BlockSpec((tm, k), lambda i: (i, 0)),
                  pl.BlockSpec((k, LANES), lambda i: (0, 0)),
                  pl.BlockSpec((1, LANES), lambda i: (0, 0)),
                  pl.BlockSpec((1, LANES), lambda i: (0, 0))],
        out_specs=pl.BlockSpec((tm, LANES), lambda i: (i, 0)),
        out_shape=jax.ShapeDtypeStruct((m, LANES), F32),
        compiler_params=_params("parallel"),
        name="dn_gates",
    )(x, w, a, dt)


GDN_BLOCK = 512
GDN_HEADS_PER_STEP = 8
CARRY = 8


def _bdot(a, b):
    return lax.dot_general(a, b, (((2,), (1,)), ((0,), (0,))), preferred_element_type=F32)


def _bdot_nt(a, b):
    return lax.dot_general(a, b, (((2,), (2,)), ((0,), (0,))), preferred_element_type=F32)


def _unit_lower_inverse(lmat):
    c = lmat.shape[-1]
    ri = lax.broadcasted_iota(jnp.int32, lmat.shape, 1)
    ci = lax.broadcasted_iota(jnp.int32, lmat.shape, 2)
    eye = (ri == ci).astype(F32)
    s = 1
    minv = eye
    while s < c:
        shift = s.bit_length()
        off = ((ri >> shift) == (ci >> shift)) & ((ri & s) != 0) & ((ci & s) == 0)
        boff = jnp.where(off, lmat, 0.0)
        if s == 1:
            minv = eye - boff
        else:
            mb = minv.astype(BF16)
            minv = minv - _bdot(_bdot(mb, boff.astype(BF16)).astype(BF16), mb)
        s *= 2
    return minv


def _gdn_kernel(q_ref, k_ref, v_ref, z_ref, gb_ref, cwq_ref, cwk_ref, cwv_ref, nw_ref, o_ref,
                ext_ref, qkv_ref, gbc_ref, state_ref, u_ref, wq_ref, intra_ref, kd_ref, egl_ref, *, heads):
    hg = pl.program_id(1)
    t = pl.program_id(2)
    tb = q_ref.shape[0]
    dh = DN_HEAD_DIM
    c = DN_CHUNK
    nchunks = tb // c

    @pl.when(t == 0)
    def _():
        state_ref[...] = jnp.zeros_like(state_ref)
        ext_ref[:, 0:CARRY, :] = jnp.zeros((3, CARRY, ext_ref.shape[2]), F32)

    gb = gb_ref[...]
    lane = lax.broadcasted_iota(jnp.int32, gb.shape, 1)
    hh = heads // 2
    nb = hh * nchunks
    ri = lax.broadcasted_iota(jnp.int32, (nb, c, c), 1)
    ci = lax.broadcasted_iota(jnp.int32, (nb, c, c), 2)

    for half in range(2):
        cols = slice(half * hh * dh, (half + 1) * hh * dh)
        hs = slice(half * hh, (half + 1) * hh)

        for idx, (ref, cw_ref) in enumerate(((q_ref, cwq_ref), (k_ref, cwk_ref), (v_ref, cwv_ref))):
            ext_ref[idx, CARRY:CARRY + tb, cols] = ref[:, cols].astype(F32)
            acc = ext_ref[idx, CARRY:CARRY + tb, cols] * cw_ref[DN_CONV - 1:DN_CONV, cols]
            for back in range(1, DN_CONV):
                acc = acc + (ext_ref[idx, CARRY - back:CARRY - back + tb, cols]
                             * cw_ref[DN_CONV - 1 - back:DN_CONV - back, cols])
            ext_ref[idx, 0:CARRY, cols] = ext_ref[idx, tb:tb + CARRY, cols]
            act = acc * _sigmoid(acc)
            for g in range(hh):
                a = act[:, g * dh:(g + 1) * dh]
                if idx < 2:
                    a = a * lax.rsqrt(jnp.sum(a * a, axis=-1, keepdims=True) + NORM_EPS)
                if idx == 0:
                    a = a * (dh ** -0.5)
                qkv_ref[idx, half * hh + g] = a.reshape(nchunks, c, dh)

        for g in range(half * hh, (half + 1) * hh):
            h = hg * heads + g
            gcol = jnp.sum(jnp.where(lane == h, gb, 0.0), axis=-1, keepdims=True)
            bcol = jnp.sum(jnp.where(lane == h + DN_HEADS, gb, 0.0), axis=-1, keepdims=True)
            gbc_ref[0, g] = jnp.broadcast_to(gcol, (tb, dh)).reshape(nchunks, c, dh)
            gbc_ref[1, g] = jnp.broadcast_to(bcol, (tb, dh)).reshape(nchunks, c, dh)

        q = qkv_ref[0, hs].reshape(nb, c, dh)
        k = qkv_ref[1, hs].reshape(nb, c, dh)
        v = qkv_ref[2, hs].reshape(nb, c, dh)
        gc = gbc_ref[0, hs].reshape(nb, c, dh)
        beta = gbc_ref[1, hs].reshape(nb, c, dh)
        gsq = gc[:, :, :c]
        diff = gsq - jnp.swapaxes(gsq, 1, 2)
        decay = jnp.where(ri >= ci, jnp.exp(jnp.minimum(diff, 0.0)), 0.0)
        kb = k * beta
        both = _bdot_nt(jnp.concatenate([q, kb], axis=1).astype(BF16), k.astype(BF16))
        lower = jnp.where(ri > ci, both[:, c:] * decay, 0.0)
        tinv = _unit_lower_inverse(lower)
        eg = jnp.exp(gc)
        rhs = jnp.concatenate([v * beta, kb * eg], axis=2).astype(BF16)
        uw = _bdot(tinv.astype(BF16), rhs)
        u_ref[hs] = uw[:, :, :dh].reshape(hh, nchunks, c, dh)
        wq_ref[hs] = jnp.concatenate([uw[:, :, dh:], q * eg], axis=1).astype(BF16).reshape(hh, nchunks, 2 * c, dh)
        intra_ref[hs] = (both[:, :c] * decay).astype(BF16).reshape(hh, nchunks, c, c)
        g_last = gc[:, c - 1:c, :]
        kd_ref[hs] = jnp.swapaxes(k * jnp.exp(g_last - gc), 1, 2).astype(BF16).reshape(hh, nchunks, dh, c)
        egl_ref[hs] = jnp.broadcast_to(jnp.exp(g_last), (nb, CARRY, dh)).reshape(hh, nchunks, CARRY, dh)

    def chunk_body(ic, carry):
        state = state_ref[...]
        ws_qs = _bdot(wq_ref[:, ic], state.astype(BF16))
        vnb = (u_ref[:, ic] - ws_qs[:, :c]).astype(BF16)
        out = ws_qs[:, c:] + _bdot(intra_ref[:, ic], vnb)
        state_ref[...] = state * egl_ref[:, ic, 0:1, :] + _bdot(kd_ref[:, ic], vnb)
        u_ref[:, ic] = out
        return carry

    lax.fori_loop(0, nchunks, chunk_body, 0)

    nw = nw_ref[...]
    for g in range(heads):
        o = u_ref[g].reshape(tb, dh)
        o = o * lax.rsqrt(jnp.mean(o * o, axis=-1, keepdims=True) + NORM_EPS) * nw
        zz = z_ref[:, g * dh:(g + 1) * dh].astype(F32)
        o_ref[:, g * dh:(g + 1) * dh] = (o * (zz * _sigmoid(zz))).astype(o_ref.dtype)


def _gdn(qkvz, gb, conv_w, dn_norm_w, batch, seq):
    tb = GDN_BLOCK
    heads = GDN_HEADS_PER_STEP
    wblk = heads * DN_HEAD_DIM
    nhg = DN_HEADS // heads
    nblk = seq // tb
    nchunks = tb // DN_CHUNK
    tokens = batch * seq
    dh = DN_HEAD_DIM

    def col(part):
        return pl.BlockSpec((tb, wblk), lambda b, h, t: (b * nblk + t, part * nhg + h))

    def cw(part):
        return pl.BlockSpec((DN_CONV, wblk), lambda b, h, t: (0, part * nhg + h))

    kern = functools.partial(_gdn_kernel, heads=heads)
    return pl.pallas_call(
        kern,
        grid=(batch, nhg, nblk),
        in_specs=[col(0), col(1), col(2), col(3),
                  pl.BlockSpec((tb, LANES), lambda b, h, t: (b * nblk + t, 0)),
                  cw(0), cw(1), cw(2),
                  pl.BlockSpec((1, dh), lambda b, h, t: (0, 0))],
        out_specs=pl.BlockSpec((tb, wblk), lambda b, h, t: (b * nblk + t, h)),
        out_shape=jax.ShapeDtypeStruct((tokens, DN_WIDTH), BF16),
        scratch_shapes=[pltpu.VMEM((3, tb + CARRY, wblk), F32),
                        pltpu.VMEM((3, heads, nchunks, DN_CHUNK, dh), F32),
                        pltpu.VMEM((2, heads, nchunks, DN_CHUNK, dh), F32),
                        pltpu.VMEM((heads, dh, dh), F32),
                        pltpu.VMEM((heads, nchunks, DN_CHUNK, dh), F32),
                        pltpu.VMEM((heads, nchunks, 2 * DN_CHUNK, dh), BF16),
                        pltpu.VMEM((heads, nchunks, DN_CHUNK, DN_CHUNK), BF16),
                        pltpu.VMEM((heads, nchunks, dh, DN_CHUNK), BF16),
                        pltpu.VMEM((heads, nchunks, CARRY, dh), F32)],
        compiler_params=_params("parallel", "parallel", "arbitrary"),
        name="gdn",
    )(qkvz, qkvz, qkvz, qkvz, gb, conv_w, conv_w, conv_w, dn_norm_w.reshape(1, dh).astype(F32))


ATT_SPAN = 2048
ATT_HEADS_PER_STEP = 2
ATT_WIN = 128
ATT_UNITS = 8


def _attn_kernel(*refs):
    qkv = refs[:3 * DA_GROUPS]
    y_ref = refs[3 * DA_GROUPS]
    ext = refs[3 * DA_GROUPS + 1:3 * DA_GROUPS + 1 + 2 * DA_GROUPS]
    acc_ref, m_ref, l_ref = refs[3 * DA_GROUPS + 1 + 2 * DA_GROUPS:]
    span = pl.program_id(2)
    win = ATT_WIN
    dh = DA_HEAD_DIM
    hp = ATT_HEADS_PER_STEP

    row = lax.broadcasted_iota(jnp.int32, (win, 2 * win), 0)
    colm = lax.broadcasted_iota(jnp.int32, (win, 2 * win), 1)
    band = (colm >= row) & (colm <= row + win)

    for gi, (window, dil) in enumerate(DA_PATTERNS):
        q_ref, k_ref, v_ref = qkv[3 * gi:3 * gi + 3]
        kext, vext = ext[2 * gi:2 * gi + 2]
        n_rows = ATT_SPAN // dil
        nblk = n_rows // win

        @pl.when(span == 0)
        def _():
            kext[:, 0:win, :] = jnp.zeros((dil, win, hp * dh), kext.dtype)
            vext[:, 0:win, :] = jnp.zeros((dil, win, hp * dh), vext.dtype)

        kext[:, win:win + n_rows, :] = k_ref[...]
        vext[:, win:win + n_rows, :] = v_ref[...]

        def unit_pair(it, carry, gi=gi, dil=dil, nblk=nblk, q_ref=q_ref, kext=kext, vext=vext):
            work = []
            for sub in range(ATT_UNITS):
                u = ATT_UNITS * it + sub
                r = u // nblk
                jb = u % nblk
                r0 = pl.multiple_of(jb * win, win)
                mask = band & ((jb > 0) | (span > 0) | (colm >= win))
                if dil > 1:
                    nat = pl.ds(r + jb * (win * dil), win, stride=dil)
                else:
                    nat = pl.ds(r0, win)
                for hh in range(hp):
                    cs = slice(hh * dh, (hh + 1) * dh)
                    prev = None
                    if gi > 0:
                        prev = (acc_ref[hh, nat, :], m_ref[hh, nat, :], l_ref[hh, nat, :])
                    work.append((hh, nat, mask, q_ref[r, pl.ds(r0, win), cs], kext[r, pl.ds(r0, 2 * win), cs],
                                 vext[r, pl.ds(r0, 2 * win), cs], prev))
            qb = jnp.stack([w[3] for w in work])
            kw = jnp.stack([w[4] for w in work])
            vw = jnp.stack([w[5] for w in work])
            mask = jnp.stack([w[2] for w in work])
            sc = jnp.where(mask, _bdot_nt(qb, kw), -jnp.inf)
            mx = jnp.max(sc, axis=-1, keepdims=True)
            if gi == 0:
                m_new = jnp.broadcast_to(mx, qb.shape)
                p = jnp.exp(sc - mx)
                l_new = jnp.broadcast_to(jnp.sum(p, axis=-1, keepdims=True), qb.shape)
                acc = _bdot(p.astype(BF16), vw)
            else:
                acc_prev = jnp.stack([w[6][0] for w in work])
                m_prev = jnp.stack([w[6][1] for w in work])
                l_prev = jnp.stack([w[6][2] for w in work])
                m_new = jnp.maximum(m_prev, mx)
                alpha = jnp.exp(m_prev - m_new)
                p = jnp.exp(sc - m_new[:, :, 0:1])
                l_new = l_prev * alpha + jnp.sum(p, axis=-1, keepdims=True)
                acc = acc_prev * alpha + _bdot(p.astype(BF16), vw)
            for i, (hh, nat) in enumerate((w[0], w[1]) for w in work):
                acc_ref[hh, nat, :] = acc[i]
                m_ref[hh, nat, :] = m_new[i]
                l_ref[hh, nat, :] = l_new[i]
            return carry

        lax.fori_loop(0, dil * nblk // ATT_UNITS, unit_pair, 0)

        kext[:, 0:win, :] = kext[:, n_rows:n_rows + win, :]
        vext[:, 0:win, :] = vext[:, n_rows:n_rows + win, :]

    for hh in range(hp):
        y_ref[:, hh * dh:(hh + 1) * dh] = (acc_ref[hh] / l_ref[hh]).astype(y_ref.dtype)


def _attention(streams, batch, seq):
    hp = ATT_HEADS_PER_STEP
    wblk = hp * DA_HEAD_DIM
    nhp = DA_HEADS_PER_GROUP // hp
    nspan = seq // ATT_SPAN
    in_specs, args, scratch = [], [], []
    for gi, (window, dil) in enumerate(DA_PATTERNS):
        n_rows = ATT_SPAN // dil
        for part in range(3):
            in_specs.append(pl.BlockSpec((None, dil, n_rows, wblk),
                                         lambda b, h, s, part=part: (b, 0, s, part * nhp + h)))
            args.append(streams[gi])
        scratch += [pltpu.VMEM((dil, n_rows + ATT_WIN, wblk), BF16)] * 2
    scratch += [pltpu.VMEM((hp, ATT_SPAN, DA_HEAD_DIM), F32)] * 3
    return pl.pallas_call(
        _attn_kernel,
        grid=(batch, nhp, nspan),
        in_specs=in_specs,
        out_specs=pl.BlockSpec((ATT_SPAN, wblk), lambda b, h, s: (b * nspan + s, h)),
        out_shape=jax.ShapeDtypeStruct((batch * seq, DA_WIDTH), BF16),
        scratch_shapes=scratch,
        compiler_params=_params("parallel", "parallel", "arbitrary"),
        name="dilated_attn",
    )(*args)


def _merge_kernel(ya_ref, yb_ref, wa_ref, wb_ref, ga_ref, gb_ref, o_ref):
    a = _dot(ya_ref[...], wa_ref[...])
    b = _dot(yb_ref[...], wb_ref[...])
    o_ref[...] = (_sigmoid(ga_ref[...].astype(F32)) * a + _sigmoid(gb_ref[...].astype(F32)) * b).astype(o_ref.dtype)


def _merge(ya, yb, wa, wb, gates, tm=1024, tn=1024):
    m = ya.shape[0]
    n = wa.shape[1]
    nt = n // tn
    return pl.pallas_call(
        _merge_kernel,
        grid=(m // tm, nt),
        in_specs=[pl.BlockSpec((tm, ya.shape[1]), lambda i, j: (i, 0)),
                  pl.BlockSpec((tm, yb.shape[1]), lambda i, j: (i, 0)),
                  pl.BlockSpec((wa.shape[0], tn), lambda i, j: (0, j)),
                  pl.BlockSpec((wb.shape[0], tn), lambda i, j: (0, j)),
                  pl.BlockSpec((tm, tn), lambda i, j: (i, j)),
                  pl.BlockSpec((tm, tn), lambda i, j: (i, nt + j))],
        out_specs=pl.BlockSpec((tm, tn), lambda i, j: (i, j)),
        out_shape=jax.ShapeDtypeStruct((m, n), BF16),
        compiler_params=_params("parallel", "arbitrary"),
        name="merge",
    )(ya, yb, wa, wb, gates, gates)


def _layer_norm(y, g, b):
    mu = jnp.mean(y, axis=-1, keepdims=True)
    yc = y - mu
    var = jnp.mean(yc * yc, axis=-1, keepdims=True)
    return yc * lax.rsqrt(var + LN_EPS) * g + b


LN_ROWS = 64


def _outproj_kernel(m_ref, w_ref, x_ref, g_ref, b_ref, rw_ref, rb_ref, xo_ref, eid_ref, ew_ref, acc_ref):
    acc_ref[...] = _dot(m_ref[...], w_ref[...])
    gam = g_ref[...]
    bet = b_ref[...]

    def ln_rows(i, carry):
        rows = pl.ds(pl.multiple_of(i * LN_ROWS, LN_ROWS), LN_ROWS)
        xo_ref[rows, :] = _layer_norm(DEEPNORM_ALPHA * x_ref[rows, :] + acc_ref[rows, :], gam, bet)
        return carry

    lax.fori_loop(0, x_ref.shape[0] // LN_ROWS, ln_rows, 0)
    logits = _dot_split(_split_bf16(xo_ref[...]), _split_bf16(rw_ref[...])) + rb_ref[...]
    lane = lax.broadcasted_iota(jnp.int32, logits.shape, 1)
    work = jnp.where(lane < N_EXPERTS, logits, -jnp.inf)
    top = jnp.max(work, axis=-1, keepdims=True)
    ids = jnp.zeros(logits.shape, jnp.int32)
    vals = jnp.full(logits.shape, -jnp.inf, F32)
    for kk in range(TOP_K):
        mx = jnp.max(work, axis=-1, keepdims=True)
        first_idx = jnp.min(jnp.where(work == mx, lane, LANES), axis=-1, keepdims=True)
        ids = jnp.where(lane == kk, first_idx, ids)
        vals = jnp.where(lane == kk, mx, vals)
        work = jnp.where(lane == first_idx, -jnp.inf, work)
    e = jnp.exp(vals - top)
    eid_ref[...] = ids
    ew_ref[...] = e / jnp.sum(e, axis=-1, keepdims=True)


def _outproj_ln_router(merged, w_out, x, ln_g, ln_b, router_w, router_b, tm=512):
    m, d = x.shape
    rw = jnp.pad(router_w.astype(F32), ((0, 0), (0, LANES - N_EXPERTS)))
    rb = jnp.pad(router_b.astype(F32), (0, LANES - N_EXPERTS)).reshape(1, LANES)
    row = lambda i: (i, 0)
    fixed = lambda i: (0, 0)
    return pl.pallas_call(
        _outproj_kernel,
        grid=(m // tm,),
        in_specs=[pl.BlockSpec((tm, d), row),
                  pl.BlockSpec((d, d), fixed),
                  pl.BlockSpec((tm, d), row),
                  pl.BlockSpec((1, d), fixed),
                  pl.BlockSpec((1, d), fixed),
                  pl.BlockSpec((d, LANES), fixed),
                  pl.BlockSpec((1, LANES), fixed)],
        out_specs=[pl.BlockSpec((tm, d), row), pl.BlockSpec((tm, LANES), row), pl.BlockSpec((tm, LANES), row)],
        out_shape=[jax.ShapeDtypeStruct((m, d), F32), jax.ShapeDtypeStruct((m, LANES), jnp.int32),
                   jax.ShapeDtypeStruct((m, LANES), F32)],
        scratch_shapes=[pltpu.VMEM((tm, d), F32)],
        compiler_params=_params("parallel"),
        name="outproj_ln_router",
    )(merged, w_out, x, ln_g.reshape(1, d).astype(F32), ln_b.reshape(1, d).astype(F32), rw, rb)


MOE_TILE = 512
MOE_TOKENS_PER_STEP = 256
ROW_SUB = D_MODEL // LANES


def _moe_rows(tokens):
    return tokens * TOP_K + N_EXPERTS * MOE_TILE


def _moe_plan(eid, tokens):
    flat = eid[:, :TOP_K].reshape(-1)
    onehot = (flat[:, None] == jnp.arange(N_EXPERTS, dtype=jnp.int32)[None, :]).astype(jnp.int32)
    csum = jnp.cumsum(onehot, axis=0)
    counts = csum[-1]
    pcount = ((counts + MOE_TILE - 1) // MOE_TILE) * MOE_TILE
    pend = jnp.cumsum(pcount)
    start = pend - pcount
    slot = jnp.sum(onehot * (start[None, :] + csum - 1), axis=1)
    ntiles = _moe_rows(tokens) // MOE_TILE
    tile_start = jnp.arange(ntiles, dtype=jnp.int32) * MOE_TILE
    tile_valid = (tile_start < pend[-1]).astype(jnp.int32)
    last_start = jnp.maximum(pend[-1] - MOE_TILE, 0)
    tile_e = jnp.sum((jnp.minimum(tile_start, last_start)[:, None] >= pend[None, :]).astype(jnp.int32), axis=1)
    nstep = tokens // MOE_TOKENS_PER_STEP
    return slot.astype(jnp.int32).reshape(nstep, 1, MOE_TOKENS_PER_STEP * TOP_K), tile_e, tile_valid


def _row(i):
    return pl.ds(pl.multiple_of(i * ROW_SUB, ROW_SUB), ROW_SUB)


def _stage_slots(slot_ref, slot_smem, sem):
    cp = pltpu.make_async_copy(slot_ref.at[0], slot_smem, sem)
    cp.start()
    cp.wait()


def _dispatch_kernel(slot_ref, x_ref, xs_in, xs_out, x3_ref, slot_smem, sems, slot_sem):
    del xs_in
    i = pl.program_id(0)
    cur = i % 2
    tm = x_ref.shape[0]
    _stage_slots(slot_ref, slot_smem, slot_sem)
    stage = x3_ref.at[cur]
    for j in range(ROW_SUB):
        stage[pl.ds(j, tm, stride=ROW_SUB), :] = x_ref[:, j * LANES:(j + 1) * LANES]

    def issue(t, carry):
        src = stage.at[_row(t)]
        for kk in range(TOP_K):
            pltpu.make_async_copy(src, xs_out.at[_row(slot_smem[0, t * TOP_K + kk])], sems.at[cur]).start()
        return carry

    lax.fori_loop(0, tm, issue, 0, unroll=4)

    def drain(buf):
        for _ in range(TOP_K):
            pltpu.make_async_copy(x3_ref.at[buf], xs_out.at[pl.ds(0, tm * ROW_SUB)], sems.at[buf]).wait()

    @pl.when(i > 0)
    def _():
        drain(1 - cur)

    @pl.when(i == pl.num_programs(0) - 1)
    def _():
        drain(cur)


def _dispatch(slots, x, xs):
    m, d = x.shape
    tm = MOE_TOKENS_PER_STEP
    return pl.pallas_call(
        _dispatch_kernel,
        grid=(m // tm,),
        in_specs=[pl.BlockSpec((1, 1, tm * TOP_K), lambda i: (i, 0, 0)),
                  pl.BlockSpec((tm, d), lambda i: (i, 0)),
                  pl.BlockSpec(memory_space=pl.ANY)],
        out_specs=pl.BlockSpec(memory_space=pl.ANY),
        out_shape=jax.ShapeDtypeStruct(xs.shape, xs.dtype),
        scratch_shapes=[pltpu.VMEM((2, tm * ROW_SUB, LANES), F32),
                        pltpu.SMEM((1, tm * TOP_K), jnp.int32),
                        pltpu.SemaphoreType.DMA((2,)),
                        pltpu.SemaphoreType.DMA],
        input_output_aliases={2: 0},
        compiler_params=_params("arbitrary"),
        name="moe_dispatch",
    )(slots, x, xs)


def _expert_kernel(te_ref, tv_ref, xs_ref, wgu_ref, bgu_ref, wd_ref, bd_ref, ys_ref):
    i = pl.program_id(0)

    @pl.when(tv_ref[i] == 0)
    def _():
        ys_ref[...] = jnp.zeros_like(ys_ref)

    @pl.when(tv_ref[i] != 0)
    def _():
        x = jnp.concatenate([xs_ref[pl.ds(j, MOE_TILE, stride=ROW_SUB), :] for j in range(ROW_SUB)],
                            axis=1).astype(BF16)
        gu = _dot(x, wgu_ref[...]) + bgu_ref[...]
        gate = jnp.minimum(gu[:, :D_EXPERT], SWIGLU_LIMIT)
        up = jnp.clip(gu[:, D_EXPERT:], -SWIGLU_LIMIT, SWIGLU_LIMIT)
        glu = gate * _sigmoid(gate * SWIGLU_ALPHA)
        y = _dot(((up + 1.0) * glu).astype(BF16), wd_ref[...]) + bd_ref[...]
        for j in range(ROW_SUB):
            ys_ref[pl.ds(j, MOE_TILE, stride=ROW_SUB), :] = y[:, j * LANES:(j + 1) * LANES]


def _experts(tile_e, tile_valid, xs, wgu, bgu, wd, bd):
    rows = xs.shape[0]
    ne, d = wgu.shape[0], wgu.shape[1]
    tile = pl.BlockSpec((MOE_TILE * ROW_SUB, LANES), lambda i, te, tv: (i, 0))

    def per_expert(shape):
        return pl.BlockSpec((None,) + shape, lambda i, te, tv: (te[i], 0, 0))

    return pl.pallas_call(
        _expert_kernel,
        grid_spec=pltpu.PrefetchScalarGridSpec(
            num_scalar_prefetch=2,
            grid=(rows // (MOE_TILE * ROW_SUB),),
            in_specs=[tile, per_expert((d, 2 * D_EXPERT)), per_expert((1, 2 * D_EXPERT)),
                      per_expert((D_EXPERT, d)), per_expert((1, d))],
            out_specs=tile),
        out_shape=jax.ShapeDtypeStruct(xs.shape, F32),
        compiler_params=_params("arbitrary"),
        name="moe_experts",
    )(tile_e, tile_valid, xs, wgu, bgu.reshape(ne, 1, -1).astype(F32), wd, bd.reshape(ne, 1, -1).astype(F32))


def _combine_kernel(slot_ref, next_slot_ref, ew_ref, x_ref, g_ref, b_ref, ys_ref, xo_ref, xbo_ref,
                    buf_ref, slot_smem, sems, slot_sem):
    i = pl.program_id(0)
    cur = i % 2
    tm = x_ref.shape[0]

    def gather(slots_ref, buf):
        _stage_slots(slots_ref, slot_smem, slot_sem)

        def issue(t, carry):
            for kk in range(TOP_K):
                pltpu.make_async_copy(ys_ref.at[_row(slot_smem[0, t * TOP_K + kk])],
                                      buf_ref.at[buf, kk, _row(t)], sems.at[buf]).start()
            return carry

        lax.fori_loop(0, tm, issue, 0, unroll=4)

    @pl.when(i == 0)
    def _():
        gather(slot_ref, cur)

    @pl.when(i + 1 < pl.num_programs(0))
    def _():
        gather(next_slot_ref, 1 - cur)

    for kk in range(TOP_K):
        pltpu.make_async_copy(ys_ref.at[pl.ds(0, tm * ROW_SUB)], buf_ref.at[cur, kk], sems.at[cur]).wait()

    w = ew_ref[...]
    lane = lax.broadcasted_iota(jnp.int32, w.shape, 1)
    wk = [jnp.sum(jnp.where(lane == kk, w, 0.0), axis=-1, keepdims=True) for kk in range(TOP_K)]
    cols = []
    for j in range(ROW_SUB):
        rows = pl.ds(j, tm, stride=ROW_SUB)
        a = wk[0] * buf_ref[cur, 0, rows, :]
        for kk in range(1, TOP_K):
            a = a + wk[kk] * buf_ref[cur, kk, rows, :]
        cols.append(a)
    x2 = _layer_norm(DEEPNORM_ALPHA * x_ref[...] + jnp.concatenate(cols, axis=1), g_ref[...], b_ref[...])
    xo_ref[...] = x2
    xbo_ref[...] = x2.astype(xbo_ref.dtype)


def _combine_ln(slots, ew, x, ys, ln_g, ln_b):
    m, d = x.shape
    tm = MOE_TOKENS_PER_STEP
    row = lambda i: (i, 0)
    fixed = lambda i: (0, 0)
    return pl.pallas_call(
        _combine_kernel,
        grid=(m // tm,),
        in_specs=[pl.BlockSpec((1, 1, tm * TOP_K), lambda i: (i, 0, 0)),
                  pl.BlockSpec((1, 1, tm * TOP_K), lambda i: (jnp.minimum(i + 1, m // tm - 1), 0, 0)),
                  pl.BlockSpec((tm, LANES), row),
                  pl.BlockSpec((tm, d), row),
                  pl.BlockSpec((1, d), fixed),
                  pl.BlockSpec((1, d), fixed),
                  pl.BlockSpec(memory_space=pl.ANY)],
        out_specs=[pl.BlockSpec((tm, d), row), pl.BlockSpec((tm, d), row)],
        out_shape=[jax.ShapeDtypeStruct((m, d), F32), jax.ShapeDtypeStruct((m, d), BF16)],
        scratch_shapes=[pltpu.VMEM((2, TOP_K, tm * ROW_SUB, LANES), F32),
                        pltpu.SMEM((1, tm * TOP_K), jnp.int32),
                        pltpu.SemaphoreType.DMA((2,)),
                        pltpu.SemaphoreType.DMA],
        compiler_params=_params("arbitrary"),
        name="moe_combine_ln",
    )(slots, slots, ew, x, ln_g.reshape(1, d).astype(F32), ln_b.reshape(1, d).astype(F32), ys)


def _moe_ln(x, eid, ew, xs, wgu, bgu, wd, bd, ln_g, ln_b):
    slots, tile_e, tile_valid = _moe_plan(eid, x.shape[0])
    xs = _dispatch(slots, x, xs)
    ys = _experts(tile_e, tile_valid, xs, wgu, bgu, wd, bd)
    x2, x2b = _combine_ln(slots, ew, x, ys, ln_g, ln_b)
    return x2, x2b, xs


def _rope_tables(seq):
    half = DA_HEAD_DIM // 2
    inv_freq = ROPE_THETA ** (-jnp.arange(half, dtype=F32) / half)
    ang = jnp.arange(seq, dtype=F32)[:, None] * inv_freq[None, :]
    cos = jnp.cos(ang)
    sin = jnp.sin(ang)
    return jnp.concatenate([cos, cos], axis=-1), jnp.concatenate([-sin, sin], axis=-1)


def _group_qkv_weights(w_qkvb, gi):
    part_w = DA_GROUPS * DA_WIDTH
    return jnp.concatenate([w_qkvb[:, p * part_w + gi * DA_WIDTH:p * part_w + (gi + 1) * DA_WIDTH]
                            for p in range(3)], axis=1)


def _layer(x, xb, xs, batch, seq, cosf, sinf, w_in, conv_w, a_log, dt_bias, dn_norm_w, w_branch_a, w_branch_b,
           w_out, ln1_g, ln1_b, router_w, router_b, w_gate_up, b_gate_up, w_down, b_down, ln2_g, ln2_b):
    w_qkvz = w_in[:, OFF_QKVZ:OFF_AB].astype(BF16)
    w_ab = w_in[:, OFF_AB:OFF_QKVB]
    w_qkvb = w_in[:, OFF_QKVB:OFF_GATES].astype(BF16)
    w_gates = w_in[:, OFF_GATES:].astype(BF16)

    qkvz = _matmul(xb, w_qkvz, BF16)
    gb = _dn_gates(x, w_ab, a_log, dt_bias)
    streams = [_matmul_streams(xb, _group_qkv_weights(w_qkvb, gi), cosf, sinf, batch, seq, dil)
               for gi, (_, dil) in enumerate(DA_PATTERNS)]
    gates = _matmul(xb, w_gates, BF16)

    y_a = _gdn(qkvz, gb, conv_w.astype(F32), dn_norm_w, batch, seq)
    y_b = _attention(streams, batch, seq)

    merged = _merge(y_a, y_b, w_branch_a.astype(BF16), w_branch_b.astype(BF16), gates)
    x1, eid, ew = _outproj_ln_router(merged, w_out.astype(BF16), x, ln1_g, ln1_b, router_w, router_b)
    return _moe_ln(x1, eid, ew, xs, w_gate_up.astype(BF16), b_gate_up, w_down.astype(BF16), b_down, ln2_g, ln2_b)


@jax.jit
def kernel(x, w_in, conv_w, a_log, dt_bias, dn_norm_w, w_branch_a, w_branch_b, w_out, ln1_g, ln1_b,
           router_w, router_b, w_gate_up, b_gate_up, w_down, b_down, ln2_g, ln2_b):
    batch, seq, d = x.shape
    cosf, sinf = _rope_tables(seq)
    xf = x.reshape(batch * seq, d)
    xb = xf.astype(BF16)
    xs = jnp.zeros((_moe_rows(batch * seq) * ROW_SUB, LANES), F32)
    for l in range(w_in.shape[0]):
        xf, xb, xs = _layer(xf, xb, xs, batch, seq, cosf, sinf, w_in[l], conv_w[l], a_log[l], dt_bias[l], dn_norm_w[l],
                        w_branch_a[l], w_branch_b[l], w_out[l], ln1_g[l], ln1_b[l], router_w[l], router_b[l],
                        w_gate_up[l], b_gate_up[l], w_down[l], b_down[l], ln2_g[l], ln2_b[l])
    return xf.reshape(batch, seq, d)
```
